```python
import jax
import jax.numpy as jnp
from jax import lax
import numpy as np

D_MODEL = 1024
BATCH = 2
SEQ = 16384
DEPTH = 1

GRID_W = 64
CTX_LEN = 256
NA_HEADS = 8
NA_HEAD_DIM = 64
NA_WIDTH = NA_HEADS * NA_HEAD_DIM
NA_WIN_ROWS = 8
NA_WIN_COLS = 16
RW_HEADS = 8
RW_HEAD_DIM = 64
RW_WIDTH = RW_HEADS * RW_HEAD_DIM
MIX_WIDTH = NA_WIDTH + RW_WIDTH
DECAY_LORA = 64
AAA_LORA = 64
GATE_LORA = 128
RW_PROJ = 3 * RW_WIDTH + 2 * DECAY_LORA + AAA_LORA + GATE_LORA
IN_PROJ = 3 * NA_WIDTH + RW_PROJ
D_FF = -(-8 * D_MODEL // (3 * 256)) * 256
NORM_EPS = 1e-6
RW_GN_EPS = 64e-5

kernel_name = 'hybrid_na_rwkv7_prefix_dit_layer'


def rms_norm(x, g, eps=NORM_EPS):
    xf = x.astype(jnp.float32)
    y = xf * lax.rsqrt(jnp.mean(xf * xf, axis=-1, keepdims=True) + eps)
    return (y * g.astype(jnp.float32)).astype(x.dtype)


def modulate(h, shift, scale):
    return h * (1.0 + scale[..., None, :]) + shift[..., None, :]


def split_heads(t, n_heads):
    return t.reshape(t.shape[:-1] + (n_heads, t.shape[-1] // n_heads))


def swiglu(u, w_in, w_out):
    gate, up = jnp.split(u @ w_in, 2, axis=-1)
    return (jax.nn.silu(gate) * up) @ w_out


def na_qkv(p, q_g, k_g):
    q, k, v = jnp.split(p, 3, axis=-1)
    q = rms_norm(split_heads(q, NA_HEADS), q_g) * (NA_HEAD_DIM ** -0.5)
    k = rms_norm(split_heads(k, NA_HEADS), k_g)
    return q, k, split_heads(v, NA_HEADS)


def neighborhood_attention(q, k, v, k_ctx, v_ctx, rpb):
    B, T, H, dh = q.shape
    rows = T // GRID_W
    wr = min(NA_WIN_ROWS, rows)
    wc = NA_WIN_COLS
    qg = q.reshape(B, rows, GRID_W, H, dh).transpose(1, 0, 3, 2, 4)
    kg = k.reshape(B, rows, GRID_W, H, dh).transpose(0, 3, 1, 2, 4)
    vg = v.reshape(B, rows, GRID_W, H, dh).transpose(0, 3, 1, 2, 4)
    kc = k_ctx.transpose(0, 2, 1, 3)
    vc = v_ctx.transpose(0, 2, 1, 3)
    col = jnp.arange(GRID_W)
    col_start = jnp.clip(col - wc // 2, 0, GRID_W - wc)
    col_idx = col_start[:, None] + jnp.arange(wc)[None, :]
    dj = col_idx - col[:, None] + (NA_WIN_COLS - 1)
    n_win = wr * wc

    def row_block(args):
        i, q_row = args
        rs = jnp.clip(i - wr // 2, 0, rows - wr)
        k_rows = lax.dynamic_slice_in_dim(kg, rs, wr, axis=2)
        v_rows = lax.dynamic_slice_in_dim(vg, rs, wr, axis=2)
        k_win = k_rows[:, :, :, col_idx]
        v_win = v_rows[:, :, :, col_idx]
        di = rs + jnp.arange(wr) - i + (NA_WIN_ROWS - 1)
        bias = rpb[:, di[:, None, None], dj[None]]
        s_win = jnp.einsum('bhqd,bhrqcd->bhqrc', q_row, k_win) + bias.transpose(0, 2, 1, 3)[None]
        s_ctx = jnp.einsum('bhqd,bhld->bhql', q_row, kc)
        s = jnp.concatenate([s_win.reshape(B, H, GRID_W, n_win), s_ctx], axis=-1)
        pr = jax.nn.softmax(s.astype(jnp.float32), axis=-1).astype(v.dtype)
        p_win = pr[..., :n_win].reshape(B, H, GRID_W, wr, wc)
        return (jnp.einsum('bhqrc,bhrqcd->bhqd', p_win, v_win)
                + jnp.einsum('bhql,bhld->bhqd', pr[..., n_win:], vc))

    out = lax.map(row_block, (jnp.arange(rows), qg))
    return out.transpose(1, 0, 3, 2, 4).reshape(B, T, H * dh)


def context_attention(q, k, v):
    s = jnp.einsum('bqhd,bkhd->bhqk', q, k).astype(jnp.float32)
    pr = jax.nn.softmax(s, axis=-1).astype(v.dtype)
    o = jnp.einsum('bhqk,bkhd->bqhd', pr, v)
    return o.reshape(o.shape[:2] + (NA_WIDTH,))


def token_shift(p, mu_prev, mu_next):
    prev = jnp.pad(p[:, :-1], ((0, 0), (1, 0), (0, 0)))
    nxt = jnp.pad(p[:, 1:], ((0, 0), (0, 1), (0, 0)))
    return p + mu_prev * (prev - p) + mu_next * (nxt - p)


def rwkv_prep(p, mu_prev, mu_next, w0, w_up, a0, a_up, g_up, k_k, k_a):
    p = token_shift(p, mu_prev, mu_next).astype(jnp.float32)
    offs = np.cumsum([RW_WIDTH, RW_WIDTH, RW_WIDTH, 2 * DECAY_LORA, AAA_LORA]).tolist()
    r, k, v, wd, ad, gd = jnp.split(p, offs, axis=-1)
    wd = wd.reshape(wd.shape[:-1] + (2, DECAY_LORA))
    logw = -jax.nn.softplus(-(w0 + jnp.einsum('btdr,drc->btdc', jnp.tanh(wd), w_up))) - 0.5
    decay = jnp.exp(-jnp.exp(logw))
    a = jax.nn.sigmoid(a0 + jnp.einsum('btr,drc->btdc', ad, a_up))
    g = jax.nn.sigmoid(gd) @ g_up
    kk = split_heads(k * k_k, RW_HEADS)
    kk = kk / jnp.maximum(jnp.sqrt(jnp.sum(kk * kk, axis=-1, keepdims=True)), 1e-12)
    k_dir = k[:, :, None, :] * (1.0 + (a - 1.0) * k_a)
    return r, v, kk, decay, a, k_dir, g


def dir_inputs(feat, d):
    r, v, kk, decay, a, k_dir, _ = feat
    hs = lambda t: split_heads(t, RW_HEADS)
    return hs(r), hs(decay[:, :, d]), hs(k_dir[:, :, d]), hs(v), -kk, kk * hs(a[:, :, d])


def wkv_scan(s0, r, w, k, v, a, b, reverse):
    def step(s, inp):
        r_t, w_t, k_t, v_t, a_t, b_t = inp
        sa = jnp.einsum('bhvk,bhk->bhv', s, a_t)
        s = s * w_t[:, :, None, :] + sa[..., None] * b_t[:, :, None, :] + v_t[..., None] * k_t[:, :, None, :]
        return s, jnp.einsum('bhvk,bhk->bhv', s, r_t)
    xs = tuple(jnp.moveaxis(t, 1, 0) for t in (r, w, k, v, a, b))
    s, ys = lax.scan(step, s0, xs, reverse=reverse)
    return s, jnp.moveaxis(ys, 0, 1)


def rwkv_bonus(inp, r_k):
    r, _, k, v, _, _ = inp
    return jnp.sum(r * k * r_k, axis=-1, keepdims=True) * v


def rwkv_finish(wkv, bonus, g, ln_g, ln_b):
    mu = jnp.mean(wkv, axis=-1, keepdims=True)
    var = jnp.mean(jnp.square(wkv - mu), axis=-1, keepdims=True)
    y = (wkv - mu) * lax.rsqrt(var + RW_GN_EPS) * ln_g.reshape(RW_HEADS, RW_HEAD_DIM) + ln_b.reshape(RW_HEADS, RW_HEAD_DIM)
    y = y + bonus
    return y.reshape(y.shape[:2] + (RW_WIDTH,)) * g


def rwkv_bidir(feat, featc, r_k, ln_g, ln_b, want_ctx):
    B = feat[0].shape[0]
    s0 = jnp.zeros((B, RW_HEADS, RW_HEAD_DIM, RW_HEAD_DIM), jnp.float32)
    wkv, bonus, wkv_c, bonus_c = 0.0, 0.0, 0.0, 0.0
    for d, rev in enumerate((False, True)):
        ix = dir_inputs(feat, d)
        ic = dir_inputs(featc, d)
        s_ctx, y_c = wkv_scan(s0, *ic, reverse=rev)
        _, y_x = wkv_scan(s_ctx, *ix, reverse=rev)
        wkv = wkv + y_x
        bonus = bonus + rwkv_bonus(ix, r_k)
        if want_ctx:
            wkv_c = wkv_c + y_c
            bonus_c = bonus_c + rwkv_bonus(ic, r_k)
    o_x = rwkv_finish(wkv, bonus, feat[6], ln_g, ln_b)
    o_c = rwkv_finish(wkv_c, bonus_c, featc[6], ln_g, ln_b) if want_ctx else None
    return o_x, o_c


def setup_inputs(seed: int = 0) -> dict:
    key = jax.random.key(seed)
    ks = jax.random.split(key, 28)
    nrm = lambda k, shape, s: jax.random.normal(k, shape, jnp.float32) * s
    L = DEPTH
    return {
        'x': nrm(ks[0], (BATCH, SEQ, D_MODEL), 1.0),
        'c': nrm(ks[1], (BATCH, D_MODEL), 1.0),
        'ctx': nrm(ks[2], (BATCH, CTX_LEN, D_MODEL), 1.0),
        'c_ctx': nrm(ks[3], (D_MODEL,), 1.0),
        'norm1_g': 1.0 + nrm(ks[4], (L, D_MODEL), 0.02),
        'norm2_g': 1.0 + nrm(ks[5], (L, D_MODEL), 0.02),
        'w_ada': nrm(ks[6], (L, D_MODEL, 6 * D_MODEL), D_MODEL ** -0.5),
        'b_ada': nrm(ks[7], (L, 6 * D_MODEL), 0.01),
        'w_in': nrm(ks[8], (L, D_MODEL, IN_PROJ), D_MODEL ** -0.5),
        'na_q_g': 1.0 + nrm(ks[9], (L, NA_HEAD_DIM), 0.02),
        'na_k_g': 1.0 + nrm(ks[10], (L, NA_HEAD_DIM), 0.02),
        'na_rpb': nrm(ks[11], (L, NA_HEADS, 2 * NA_WIN_ROWS - 1, 2 * NA_WIN_COLS - 1), 0.2),
        'rw_mu_prev': jax.random.uniform(ks[12], (L, RW_PROJ), jnp.float32, 0.0, 0.5),
        'rw_mu_next': jax.random.uniform(ks[13], (L, RW_PROJ), jnp.float32, 0.0, 0.5),
        'rw_w0': jax.random.uniform(ks[14], (L, 2, RW_WIDTH), jnp.float32, -5.0, 0.0),
        'rw_w_up': nrm(ks[15], (L, 2, DECAY_LORA, RW_WIDTH), 0.5 * DECAY_LORA ** -0.5),
        'rw_a0': nrm(ks[16], (L, 2, RW_WIDTH), 0.5),
        'rw_a_up': nrm(ks[17], (L, 2, AAA_LORA, RW_WIDTH), AAA_LORA ** -0.5),
        'rw_g_up': nrm(ks[18], (L, GATE_LORA, RW_WIDTH), GATE_LORA ** -0.5),
        'rw_k_k': 0.85 + nrm(ks[19], (L, RW_WIDTH), 0.05),
        'rw_k_a': 1.0 + nrm(ks[20], (L, RW_WIDTH), 0.05),
        'rw_r_k': nrm(ks[21], (L, RW_HEADS, RW_HEAD_DIM), 0.1),
        'rw_ln_g': 1.0 + nrm(ks[22], (L, RW_WIDTH), 0.02),
        'rw_ln_b': nrm(ks[23], (L, RW_WIDTH), 0.01),
        'w_out': nrm(ks[24], (L, MIX_WIDTH, D_MODEL), MIX_WIDTH ** -0.5),
        'ffn_w_in': nrm(ks[25], (L, D_MODEL, 2 * D_FF), D_MODEL ** -0.5),
        'ffn_w_out': nrm(ks[26], (L, D_FF, D_MODEL), D_FF ** -0.5),
    }


def reference(x, c, ctx, c_ctx, norm1_g, norm2_g, w_ada, b_ada, w_in, na_q_g, na_k_g, na_rpb,
              rw_mu_prev, rw_mu_next, rw_w0, rw_w_up, rw_a0, rw_a_up, rw_g_up, rw_k_k, rw_k_a,
              rw_r_k, rw_ln_g, rw_ln_b, w_out, ffn_w_in, ffn_w_out):
    silu_c = jax.nn.silu(c)
    silu_cc = jax.nn.silu(c_ctx)
    h, hc = x, ctx
    for l in range(DEPTH):
        last = l == DEPTH - 1
        mod = jnp.split(silu_c @ w_ada[l] + b_ada[l], 6, axis=-1)
        modc = jnp.split(silu_cc @ w_ada[l] + b_ada[l], 6, axis=-1)
        u = modulate(rms_norm(h, norm1_g[l]), mod[0], mod[1])
        uc = modulate(rms_norm(hc, norm1_g[l]), modc[0], modc[1])
        p = u @ w_in[l]
        pc = uc @ w_in[l]
        q, k, v = na_qkv(p[..., :3 * NA_WIDTH], na_q_g[l], na_k_g[l])
        qc, kc, vc = na_qkv(pc[..., :3 * NA_WIDTH], na_q_g[l], na_k_g[l])
        o_na = neighborhood_attention(q, k, v, kc, vc, na_rpb[l])
        rw_args = (rw_mu_prev[l], rw_mu_next[l], rw_w0[l], rw_w_up[l], rw_a0[l], rw_a_up[l],
                   rw_g_up[l], rw_k_k[l], rw_k_a[l])
        feat = rwkv_prep(p[..., 3 * NA_WIDTH:], *rw_args)
        featc = rwkv_prep(pc[..., 3 * NA_WIDTH:], *rw_args)
        o_rw, o_rwc = rwkv_bidir(feat, featc, rw_r_k[l], rw_ln_g[l], rw_ln_b[l], not last)
        mix = jnp.concatenate([o_na, o_rw.astype(h.dtype)], axis=-1) @ w_out[l]
        h = h + mod[2][:, None, :] * mix
        h = h + mod[5][:, None, :] * swiglu(modulate(rms_norm(h, norm2_g[l]), mod[3], mod[4]),
                                            ffn_w_in[l], ffn_w_out[l])
        if not last:
            mixc = jnp.concatenate([context_attention(qc, kc, vc), o_rwc.astype(hc.dtype)], axis=-1) @ w_out[l]
            hc = hc + modc[2] * mixc
            hc = hc + modc[5] * swiglu(modulate(rms_norm(hc, norm2_g[l]), modc[3], modc[4]),
                                       ffn_w_in[l], ffn_w_out[l])
    return h
```

```python
import functools

import numpy as np
import jax
import jax.numpy as jnp
from jax import lax
from jax.experimental import pallas as pl
from jax.experimental.pallas import tpu as pltpu

F32 = jnp.float32
BF16 = jnp.bfloat16

GRID_W = 64
NA_HEADS = 8
HEAD_DIM = 64
NA_WIDTH = NA_HEADS * HEAD_DIM
NA_WIN_ROWS = 8
NA_WIN_COLS = 16
RW_HEADS = 8
RW_WIDTH = RW_HEADS * HEAD_DIM
DECAY_LORA = 64
AAA_LORA = 64
GATE_LORA = 128
NORM_EPS = 1e-6
RW_GN_EPS = 64e-5

LANE = 128
RW_COLS = 3 * RW_WIDTH + 3 * LANE
OFF_WD = 3 * RW_WIDTH
OFF_AD = OFF_WD + LANE
OFF_GD = OFF_AD + LANE
CHUNK = 64
INV_BASE = 16
NA_ROWS_PER_STEP = 4
NEG_BIG = -1e30
VMEM_LIMIT = 56 * 1024 * 1024


def _dot(a, b):
    return jnp.dot(a, b, preferred_element_type=F32)


def _bdot(a, b):
    return jnp.dot(a.astype(BF16), b.astype(BF16), preferred_element_type=F32)


def _bdot_nt(a, b):
    return lax.dot_general(a.astype(BF16), b.astype(BF16), (((1,), (1,)), ((), ())),
                           preferred_element_type=F32)


def _bdot_tn(a, b):
    return lax.dot_general(a.astype(BF16), b.astype(BF16), (((0,), (0,)), ((), ())),
                           preferred_element_type=F32)


def _split2(a):
    hi = a.astype(BF16)
    lo = (a - hi.astype(F32)).astype(BF16)
    return hi, lo


def _split3(a):
    hi = a.astype(BF16)
    r1 = a - hi.astype(F32)
    mid = r1.astype(BF16)
    lo = (r1 - mid.astype(F32)).astype(BF16)
    return hi, mid, lo


def _dot_hl(a, w):
    hi, lo = _split2(a)
    return _dot(hi, w) + _dot(lo, w)


def _dot_hl3(a, w_hi, w_lo):
    hi, lo = _split2(a)
    return _dot(hi, w_hi) + (_dot(lo, w_hi) + _dot(hi, w_lo))


def _sigmoid(x):
    return jax.nn.sigmoid(x)


def _const_spec(shape):
    nd = len(shape)
    return pl.BlockSpec(shape, lambda *_: (0,) * nd)


def _ada_kernel(c_ref, w_ref, b_ref, o_ref):
    c = c_ref[...]
    s = c * _sigmoid(c)
    w_hi, w_lo = _split2(w_ref[...])
    o_ref[...] = _dot_hl3(s, w_hi, w_lo) + b_ref[...]


def _ada(c_rows, w_ada, b_ada):
    rows, d = c_rows.shape
    n = w_ada.shape[1]
    tn = 1024
    return pl.pallas_call(
        _ada_kernel,
        grid=(n // tn,),
        in_specs=[pl.BlockSpec((rows, d), lambda j: (0, 0)),
                  pl.BlockSpec((d, tn), lambda j: (0, j)),
                  pl.BlockSpec((1, tn), lambda j: (0, j))],
        out_specs=pl.BlockSpec((rows, tn), lambda j: (0, j)),
        out_shape=jax.ShapeDtypeStruct((rows, n), F32),
        compiler_params=pltpu.CompilerParams(dimension_semantics=("arbitrary",), vmem_limit_bytes=VMEM_LIMIT),
        name="ada",
    )(c_rows, w_ada, b_ada)


def _inproj_kernel(x_ref, sh_ref, sc_ref, g_ref, wna_ref, wrw_ref, qg_ref, kg_ref, ones_ref,
                   q_ref, k_ref, v_ref, prw_ref):
    x = x_ref[0]
    ms = jnp.mean(x * x, axis=-1, keepdims=True)
    y = x * lax.rsqrt(ms + NORM_EPS) * g_ref[...]
    u = (y * (1.0 + sc_ref[0]) + sh_ref[0]).astype(BF16)
    pn = _dot(u, wna_ref[...])

    def head_norm(t, gain):
        ss = _dot_hl(t * t, ones_ref[...]) * (1.0 / HEAD_DIM)
        return t * lax.rsqrt(ss + NORM_EPS) * gain

    q = head_norm(pn[:, :NA_WIDTH], qg_ref[...]) * (HEAD_DIM ** -0.5)
    k = head_norm(pn[:, NA_WIDTH:2 * NA_WIDTH], kg_ref[...])
    q_ref[0] = q.astype(BF16)
    k_ref[0] = k.astype(BF16)
    v_ref[0] = pn[:, 2 * NA_WIDTH:].astype(BF16)
    prw_ref[0] = _dot(u, wrw_ref[...])


def _inproj(x, shift, scale, g1, w_na, w_rw, qg, kg, ones_bd):
    b, t, d = x.shape
    tm = min(512, t)
    vec = pl.BlockSpec((1, 1, d), lambda bi, i: (bi, 0, 0))
    tok = lambda w: pl.BlockSpec((1, tm, w), lambda bi, i: (bi, i, 0))
    return pl.pallas_call(
        _inproj_kernel,
        grid=(b, t // tm),
        in_specs=[tok(d), vec, vec, _const_spec((1, d)), _const_spec(w_na.shape), _const_spec(w_rw.shape),
                  _const_spec((1, NA_WIDTH)), _const_spec((1, NA_WIDTH)), _const_spec(ones_bd.shape)],
        out_specs=[tok(NA_WIDTH), tok(NA_WIDTH), tok(NA_WIDTH), tok(RW_COLS)],
        out_shape=[jax.ShapeDtypeStruct((b, t, NA_WIDTH), BF16)] * 3
        + [jax.ShapeDtypeStruct((b, t, RW_COLS), F32)],
        compiler_params=pltpu.CompilerParams(dimension_semantics=("arbitrary", "arbitrary"),
                                             vmem_limit_bytes=VMEM_LIMIT),
        name="inproj",
    )(x, shift, scale, g1, w_na, w_rw, qg, kg, ones_bd)


def _na_bias_tables(rpb, rows, r_blk):
    kr_n = r_blk + NA_WIN_ROWS - 1
    nblk = rows // r_blk
    j = np.arange(GRID_W)
    cs = np.clip(j - NA_WIN_COLS // 2, 0, GRID_W - NA_WIN_COLS)
    c = np.arange(GRID_W)
    cvalid = (c[None, :] >= cs[:, None]) & (c[None, :] < cs[:, None] + NA_WIN_COLS)
    dj = np.clip(c[None, :] - j[:, None] + NA_WIN_COLS - 1, 0, 2 * NA_WIN_COLS - 2)
    tabs = []
    for ib in (0, min(1, nblk - 1), nblk - 1):
        i0 = ib * r_blk
        base = int(np.clip(i0 - NA_WIN_ROWS // 2, 0, rows - kr_n))
        i = i0 + np.arange(r_blk)
        rs = np.clip(i - NA_WIN_ROWS // 2, 0, rows - NA_WIN_ROWS)
        kr = base + np.arange(kr_n)
        rvalid = (kr[None, :] >= rs[:, None]) & (kr[None, :] < rs[:, None] + NA_WIN_ROWS)
        di = np.clip(kr[None, :] - i[:, None] + NA_WIN_ROWS - 1, 0, 2 * NA_WIN_ROWS - 2)
        valid = rvalid[:, None, :, None] & cvalid[None, :, None, :]
        tab = rpb[:, di[:, None, :, None], dj[None, :, None, :]]
        tab = jnp.where(valid[None], tab, NEG_BIG)
        tabs.append(tab.reshape(rpb.shape[0], r_blk * GRID_W, kr_n * GRID_W))
    return jnp.stack(tabs)


def _na_kernel(q_ref, k_ref, v_ref, kc_ref, vc_ref, bias_ref, o_ref, *, r_blk, rows):
    i = pl.program_id(2)
    kr_n = r_blk + NA_WIN_ROWS - 1
    base = jnp.clip(i * r_blk - NA_WIN_ROWS // 2, 0, rows - kr_n)
    start = pl.multiple_of(base * GRID_W, GRID_W)
    q = q_ref[0]
    kw = k_ref[0, pl.ds(start, kr_n * GRID_W), :]
    vw = v_ref[0, pl.ds(start, kr_n * GRID_W), :]
    kc = kc_ref[0]
    vc = vc_ref[0]
    lane = lax.broadcasted_iota(jnp.int32, q.shape, 1)
    outs = []
    for hh in range(2):
        sel = (lane < HEAD_DIM) if hh == 0 else (lane >= HEAD_DIM)
        qh = jnp.where(sel, q, jnp.zeros_like(q))
        s = _bdot_nt(qh, kw) + bias_ref[0, hh]
        sc = _bdot_nt(qh, kc)
        m = jnp.maximum(jnp.max(s, axis=-1, keepdims=True), jnp.max(sc, axis=-1, keepdims=True))
        p = jnp.exp(s - m)
        pc = jnp.exp(sc - m)
        l = jnp.sum(p, axis=-1, keepdims=True) + jnp.sum(pc, axis=-1, keepdims=True)
        o = _bdot(p, vw) + _bdot(pc, vc)
        outs.append(o / l)
    o_ref[0] = jnp.where(lane < HEAD_DIM, outs[0], outs[1]).astype(o_ref.dtype)


def _na(q, k, v, kc, vc, bias):
    b, t, _ = q.shape
    l_ctx = kc.shape[1]
    rows = t // GRID_W
    r_blk = NA_ROWS_PER_STEP
    kr_n = r_blk + NA_WIN_ROWS - 1
    nblk = rows // r_blk
    assert rows % r_blk == 0 and rows >= kr_n and t % GRID_W == 0
    n_pair = NA_WIDTH // LANE

    def bias_map(bi, hp, i):
        case = jnp.where(i == 0, 0, jnp.where(i == nblk - 1, 2, 1))
        return (case, hp, 0, 0)

    return pl.pallas_call(
        functools.partial(_na_kernel, r_blk=r_blk, rows=rows),
        grid=(b, n_pair, nblk),
        in_specs=[pl.BlockSpec((1, r_blk * GRID_W, LANE), lambda bi, hp, i: (bi, i, hp)),
                  pl.BlockSpec((1, t, LANE), lambda bi, hp, i: (bi, 0, hp)),
                  pl.BlockSpec((1, t, LANE), lambda bi, hp, i: (bi, 0, hp)),
                  pl.BlockSpec((1, l_ctx, LANE), lambda bi, hp, i: (bi, 0, hp)),
                  pl.BlockSpec((1, l_ctx, LANE), lambda bi, hp, i: (bi, 0, hp)),
                  pl.BlockSpec((1, 2, r_blk * GRID_W, kr_n * GRID_W), bias_map)],
        out_specs=pl.BlockSpec((1, r_blk * GRID_W, LANE), lambda bi, hp, i: (bi, i, hp)),
        out_shape=jax.ShapeDtypeStruct((b, t, NA_WIDTH), BF16),
        compiler_params=pltpu.CompilerParams(dimension_semantics=("arbitrary",) * 3, vmem_limit_bytes=VMEM_LIMIT),
        name="na",
    )(q, k, v, kc, vc, bias)


def _softplus(z):
    return jnp.maximum(z, 0.0) + jnp.log(1.0 + jnp.exp(-jnp.abs(z)))


def _rw_pre_kernel(p_ref, pp_ref, pn_ref, mup_ref, mun_ref, w0_ref, wup_hi_ref, wup_lo_ref, a0_ref, aup_hi_ref,
                   aup_lo_ref, gup_hi_ref, gup_lo_ref, kkw_ref, ka_ref, rk_ref, ones_ref, tri_ref,
                   g_ref, bon_ref, rh_ref, y0_ref, m_ref, n_ref):
    c = pl.program_id(1)
    nc = pl.num_programs(1)
    cl = CHUNK
    p = p_ref[0]
    row = lax.broadcasted_iota(jnp.int32, (cl, 1), 0)
    halo_prev = pp_ref[0, 7:8, :] * (c > 0).astype(F32)
    halo_next = pn_ref[0, 0:1, :] * (c < nc - 1).astype(F32)
    prev = jnp.where(row == 0, halo_prev, pltpu.roll(p, 1, 0))
    nxt = jnp.where(row == cl - 1, halo_next, pltpu.roll(p, cl - 1, 0))
    ps = p + mup_ref[...] * (prev - p) + mun_ref[...] * (nxt - p)

    r = ps[:, 0:RW_WIDTH]
    k = ps[:, RW_WIDTH:2 * RW_WIDTH]
    v = ps[:, 2 * RW_WIDTH:3 * RW_WIDTH]
    wd = ps[:, OFF_WD:OFF_WD + LANE]
    ad = ps[:, OFF_AD:OFF_AD + LANE]
    gd = ps[:, OFF_GD:OFF_GD + LANE]

    lora_w = _dot_hl3(jnp.tanh(wd), wup_hi_ref[...], wup_lo_ref[...])
    lora_a = _dot_hl3(ad, aup_hi_ref[...], aup_lo_ref[...])
    g_ref[0] = _dot_hl3(_sigmoid(gd), gup_hi_ref[...], gup_lo_ref[...])

    ones = ones_ref[...]
    kkv = k * kkw_ref[...]
    kk = kkv / jnp.maximum(jnp.sqrt(_dot_hl(kkv * kkv, ones)), 1e-12)

    ri = lax.broadcasted_iota(jnp.int32, (cl, cl), 0)
    ci = lax.broadcasted_iota(jnp.int32, (cl, cl), 1)
    eye = ri == ci
    eye_f = eye.astype(F32)
    n_sq = int(np.log2(INV_BASE)) - 1
    blk = []
    size = INV_BASE
    while size <= cl:
        sh = int(np.log2(size))
        blk.append((ri >> sh) == (ci >> sh))
        size *= 2

    bonus = jnp.zeros((cl, RW_WIDTH), F32)
    for d in range(2):
        sl_d = slice(d * RW_WIDTH, (d + 1) * RW_WIDTH)
        z = w0_ref[d:d + 1, :] + lora_w[:, sl_d]
        lw = -jnp.exp(-_softplus(-z) - 0.5)
        a = _sigmoid(a0_ref[d:d + 1, :] + lora_a[:, sl_d])
        kd = k * (1.0 + (a - 1.0) * ka_ref[...])
        bvec = kk * a
        bonus = bonus + _dot_hl(r * kd * rk_ref[...], ones) * v

        l_hi, l_mid, l_lo = _split3(lw)
        tri = tri_ref[d]
        cum = _dot(tri, l_hi) + (_dot(tri, l_mid) + _dot(tri, l_lo))
        tot = jnp.sum(lw, axis=0, keepdims=True)
        e_ex = jnp.exp(cum - lw)
        e_in = jnp.exp(cum)
        e_neg = jnp.exp(-cum)
        e_tot = jnp.exp(tot - cum)
        p_tot = jnp.exp(tot)
        at_all = -kk * e_ex
        rt_all = r * e_in
        bt_all = bvec * e_neg
        kt_all = kd * e_neg
        bb_all = bvec * e_tot
        kb_all = kd * e_tot
        strict = (ci < ri) if d == 0 else (ci > ri)
        incl = (ci <= ri) if d == 0 else (ci >= ri)
        for h in range(RW_HEADS):
            sl = slice(h * HEAD_DIM, (h + 1) * HEAD_DIM)
            at, rt, bt, kt = at_all[:, sl], rt_all[:, sl], bt_all[:, sl], kt_all[:, sl]
            vh = v[:, sl]
            aab = jnp.where(strict, _bdot_nt(at, bt), 0.0)
            aak = jnp.where(strict, _bdot_nt(at, kt), 0.0)
            arb = jnp.where(incl, _bdot_nt(rt, bt), 0.0)
            ark = jnp.where(incl, _bdot_nt(rt, kt), 0.0)
            d0 = jnp.where(blk[0], aab, 0.0)
            tinv = d0 + eye_f
            apow = d0
            for _ in range(n_sq):
                apow = _bdot(apow, apow)
                tinv = tinv + _bdot(tinv, apow)
            for lvl in range(1, len(blk)):
                off = jnp.where(jnp.logical_and(blk[lvl], jnp.logical_not(blk[lvl - 1])), aab, 0.0)
                tinv = tinv + _bdot(tinv, _bdot(off, tinv))
            ah = _bdot(tinv, at)
            wh = _bdot(tinv, _bdot(aak, vh))
            rh_ref[d, 0, :, sl] = rt + _bdot(arb, ah)
            y0_ref[d, 0, :, sl] = _bdot(arb, wh) + _bdot(ark, vh)
            bb, kb = bb_all[:, sl], kb_all[:, sl]
            m_ref[d, 0, 0, h] = jnp.where(eye, p_tot[:, sl], 0.0) + _bdot_tn(bb, ah)
            n_ref[d, 0, 0, h] = _bdot_tn(bb, wh) + _bdot_tn(kb, vh)
    bon_ref[0] = bonus


def _rw_pre(prw, consts):
    b, t, _ = prw.shape
    cl = CHUNK
    nc = t // cl
    assert t % cl == 0 and cl % 8 == 0
    sub = cl // 8
    nb8 = t // 8
    tok = lambda w: pl.BlockSpec((1, cl, w), lambda bi, c: (bi, c, 0))
    in_specs = [tok(RW_COLS),
                pl.BlockSpec((1, 8, RW_COLS), lambda bi, c: (bi, jnp.maximum(c * sub - 1, 0), 0)),
                pl.BlockSpec((1, 8, RW_COLS), lambda bi, c: (bi, jnp.minimum((c + 1) * sub, nb8 - 1), 0))]
    in_specs += [_const_spec(a.shape) for a in consts]
    dirtok = pl.BlockSpec((2, 1, cl, RW_WIDTH), lambda bi, c: (0, bi, c, 0))
    mat = pl.BlockSpec((2, 1, 1, RW_HEADS, HEAD_DIM, HEAD_DIM), lambda bi, c: (0, bi, c, 0, 0, 0))
    return pl.pallas_call(
        _rw_pre_kernel,
        grid=(b, nc),
        in_specs=in_specs,
        out_specs=[tok(RW_WIDTH), tok(RW_WIDTH), dirtok, dirtok, mat, mat],
        out_shape=[jax.ShapeDtypeStruct((b, t, RW_WIDTH), F32)] * 2
        + [jax.ShapeDtypeStruct((2, b, t, RW_WIDTH), F32)] * 2
        + [jax.ShapeDtypeStruct((2, b, nc, RW_HEADS, HEAD_DIM, HEAD_DIM), F32)] * 2,
        compiler_params=pltpu.CompilerParams(dimension_semantics=("arbitrary", "arbitrary"),
                                             vmem_limit_bytes=VMEM_LIMIT),
        name="rw_pre",
    )(prw, prw, prw, *consts)


def _rw_scan_kernel(rhf_ref, y0f_ref, mf_ref, nf_ref, rhr_ref, y0r_ref, mr_ref, nr_ref, z0_ref,
                    yf_ref, yr_ref, zf_ref, z_scr):
    j = pl.program_id(0)

    @pl.when(j == 0)
    def _():
        z_scr[...] = z0_ref[...]

    nb = z_scr.shape[1]
    dirs = ((rhf_ref, y0f_ref, mf_ref, nf_ref, yf_ref), (rhr_ref, y0r_ref, mr_ref, nr_ref, yr_ref))
    for d, (rh_ref, y0_ref, m_ref, n_ref, y_ref) in enumerate(dirs):
        for bi in range(nb):
            for h in range(RW_HEADS):
                sl = slice(h * HEAD_DIM, (h + 1) * HEAD_DIM)
                z = z_scr[d, bi, h]
                y_ref[bi, :, sl] = _bdot(rh_ref[0, bi, :, sl], z) + y0_ref[0, bi, :, sl]
                z_scr[d, bi, h] = _bdot(m_ref[0, bi, 0, h], z) + n_ref[0, bi, 0, h]

    @pl.when(j == pl.num_programs(0) - 1)
    def _():
        zf_ref[...] = z_scr[...]


def _rw_scan(rh, y0, m, n, z0):
    _, b, t, _ = rh.shape
    cl = CHUNK
    nc = t // cl
    tokf = pl.BlockSpec((1, b, cl, RW_WIDTH), lambda j: (0, 0, j, 0))
    tokr = pl.BlockSpec((1, b, cl, RW_WIDTH), lambda j: (1, 0, nc - 1 - j, 0))
    matf = pl.BlockSpec((1, b, 1, RW_HEADS, HEAD_DIM, HEAD_DIM), lambda j: (0, 0, j, 0, 0, 0))
    matr = pl.BlockSpec((1, b, 1, RW_HEADS, HEAD_DIM, HEAD_DIM), lambda j: (1, 0, nc - 1 - j, 0, 0, 0))
    zspec = _const_spec(z0.shape)
    return pl.pallas_call(
        _rw_scan_kernel,
        grid=(nc,),
        in_specs=[tokf, tokf, matf, matf, tokr, tokr, matr, matr, zspec],
        out_specs=[pl.BlockSpec((b, cl, RW_WIDTH), lambda j: (0, j, 0)),
                   pl.BlockSpec((b, cl, RW_WIDTH), lambda j: (0, nc - 1 - j, 0)),
                   zspec],
        out_shape=[jax.ShapeDtypeStruct((b, t, RW_WIDTH), F32)] * 2 + [jax.ShapeDtypeStruct(z0.shape, F32)],
        scratch_shapes=[pltpu.VMEM(z0.shape, F32)],
        compiler_params=pltpu.CompilerParams(dimension_semantics=("arbitrary",), vmem_limit_bytes=VMEM_LIMIT),
        name="rw_scan",
    )(rh, y0, m, n, rh, y0, m, n, z0)


def _tail_kernel(yf_ref, yr_ref, bon_ref, g_ref, ona_ref, x_ref, gate1_ref, sh_ref, sc_ref, gate2_ref,
                 lng_ref, lnb_ref, ones_ref, wona_ref, worw_ref, g2_ref, w1g_ref, w1u_ref, w2_ref, o_ref, *, ff_chunk):
    ones = ones_ref[...]
    wkv = yf_ref[0] + yr_ref[0]
    mu = _dot_hl(wkv, ones) * (1.0 / HEAD_DIM)
    dlt = wkv - mu
    var = _dot_hl(dlt * dlt, ones) * (1.0 / HEAD_DIM)
    yn = dlt * lax.rsqrt(var + RW_GN_EPS) * lng_ref[...] + lnb_ref[...]
    orw = ((yn + bon_ref[0]) * g_ref[0]).astype(BF16)
    mix = _dot(ona_ref[0], wona_ref[...]) + _dot(orw, worw_ref[...])
    h1 = x_ref[0] + gate1_ref[0] * mix
    ms = jnp.mean(h1 * h1, axis=-1, keepdims=True)
    u = (h1 * lax.rsqrt(ms + NORM_EPS) * g2_ref[...] * (1.0 + sc_ref[0]) + sh_ref[0]).astype(BF16)
    d_ff = w2_ref.shape[0]
    acc = jnp.zeros(h1.shape, F32)
    for ci in range(d_ff // ff_chunk):
        sl = slice(ci * ff_chunk, (ci + 1) * ff_chunk)
        gt = _dot(u, w1g_ref[:, sl])
        up = _dot(u, w1u_ref[:, sl])
        act = (gt * _sigmoid(gt) * up).astype(BF16)
        acc = acc + _dot(act, w2_ref[sl, :])
    o_ref[0] = h1 + gate2_ref[0] * acc


def _ff_chunk(d_ff):
    best = LANE
    for cand in range(LANE, d_ff + 1, LANE):
        if d_ff % cand == 0 and cand <= 1536:
            best = cand
    return best


def _tail(yf, yr, bon, g, ona, x, gate1, sh, sc, gate2, lng, lnb, ones_bd, wona, worw, g2, w1g, w1u, w2):
    b, t, d = x.shape
    tm = min(256, t)
    vec = pl.BlockSpec((1, 1, d), lambda bi, i: (bi, 0, 0))
    tok = lambda w: pl.BlockSpec((1, tm, w), lambda bi, i: (bi, i, 0))
    consts = (lng, lnb, ones_bd, wona, worw, g2, w1g, w1u, w2)
    return pl.pallas_call(
        functools.partial(_tail_kernel, ff_chunk=_ff_chunk(w2.shape[0])),
        grid=(b, t // tm),
        in_specs=[tok(RW_WIDTH)] * 4 + [tok(NA_WIDTH), tok(d), vec, vec, vec, vec]
        + [_const_spec(a.shape) for a in consts],
        out_specs=tok(d),
        out_shape=jax.ShapeDtypeStruct((b, t, d), F32),
        compiler_params=pltpu.CompilerParams(dimension_semantics=("arbitrary", "arbitrary"),
                                             vmem_limit_bytes=VMEM_LIMIT),
        name="tail",
    )(yf, yr, bon, g, ona, x, gate1, sh, sc, gate2, *consts)


def _pad_cols(w, width):
    return jnp.pad(w, ((0, 0), (0, width - w.shape[1])))


def _rw_layout(w):
    o = 3 * RW_WIDTH
    wd = w[:, o:o + 2 * DECAY_LORA]
    ad = _pad_cols(w[:, o + 2 * DECAY_LORA:o + 2 * DECAY_LORA + AAA_LORA], LANE)
    gd = w[:, o + 2 * DECAY_LORA + AAA_LORA:]
    return jnp.concatenate([w[:, :o], wd, ad, gd], axis=1)


def _hl(w):
    hi = w.astype(BF16)
    return hi, (w - hi.astype(F32)).astype(BF16)


def kernel(x, c, ctx, c_ctx, norm1_g, norm2_g, w_ada, b_ada, w_in, na_q_g, na_k_g, na_rpb, rw_mu_prev, rw_mu_next,
           rw_w0, rw_w_up, rw_a0, rw_a_up, rw_g_up, rw_k_k, rw_k_a, rw_r_k, rw_ln_g, rw_ln_b, w_out, ffn_w_in,
           ffn_w_out):
    depth = w_in.shape[0]
    assert depth == 1, "single-layer kernel"
    b, t, d = x.shape
    l_ctx = ctx.shape[1]
    lyr = 0

    n_rows = -(-(b + 1) // 8) * 8
    c_rows = jnp.zeros((n_rows, d), F32).at[:b].set(c).at[b].set(c_ctx)
    mod_all = _ada(c_rows, w_ada[lyr], b_ada[lyr][None, :])
    mod = [mod_all[:b, i * d:(i + 1) * d][:, None, :] for i in range(6)]
    modc = [jnp.broadcast_to(mod_all[b, i * d:(i + 1) * d][None, None, :], (b, 1, d)) for i in range(2)]

    w_na = w_in[lyr][:, :3 * NA_WIDTH].astype(BF16)
    w_rw = _rw_layout(w_in[lyr][:, 3 * NA_WIDTH:]).astype(BF16)
    hd = np.arange(NA_WIDTH) // HEAD_DIM
    ones_bd = jnp.asarray(hd[:, None] == hd[None, :], BF16)
    qg = jnp.tile(na_q_g[lyr], NA_HEADS)[None, :]
    kg = jnp.tile(na_k_g[lyr], NA_HEADS)[None, :]
    g1 = norm1_g[lyr][None, :]

    q, k, v, prw = _inproj(x, mod[0], mod[1], g1, w_na, w_rw, qg, kg, ones_bd)
    _, kc, vc, prw_c = _inproj(ctx, modc[0], modc[1], g1, w_na, w_rw, qg, kg, ones_bd)

    bias = _na_bias_tables(na_rpb[lyr], t // GRID_W, NA_ROWS_PER_STEP)
    o_na = _na(q, k, v, kc, vc, bias)

    mu_p = _rw_layout(rw_mu_prev[lyr][None, :])
    mu_n = _rw_layout(rw_mu_next[lyr][None, :])
    zeros_up = jnp.zeros((DECAY_LORA, RW_WIDTH), F32)
    wup = jnp.concatenate([jnp.concatenate([rw_w_up[lyr, 0], zeros_up], axis=1),
                           jnp.concatenate([zeros_up, rw_w_up[lyr, 1]], axis=1)], axis=0)
    aup = jnp.concatenate([jnp.concatenate([rw_a_up[lyr, 0], rw_a_up[lyr, 1]], axis=1),
                           jnp.zeros((LANE - AAA_LORA, 2 * RW_WIDTH), F32)], axis=0)
    ri = np.arange(CHUNK)
    tri = jnp.asarray(np.stack([ri[None, :] <= ri[:, None], ri[None, :] >= ri[:, None]]), BF16)
    consts = (mu_p, mu_n, rw_w0[lyr], *_hl(wup), rw_a0[lyr], *_hl(aup), *_hl(rw_g_up[lyr]),
              rw_k_k[lyr][None, :], rw_k_a[lyr][None, :], rw_r_k[lyr].reshape(1, RW_WIDTH), ones_bd, tri)

    _, _, rh_c, y0_c, m_c, n_c = _rw_pre(prw_c, consts)
    g, bon, rh, y0, m, n = _rw_pre(prw, consts)
    z0 = jnp.zeros((2, b, RW_HEADS, HEAD_DIM, HEAD_DIM), F32)
    _, _, z_ctx = _rw_scan(rh_c, y0_c, m_c, n_c, z0)
    yf, yr, _ = _rw_scan(rh, y0, m, n, z_ctx)

    wo = w_out[lyr].astype(BF16)
    d_ff = ffn_w_out.shape[1]
    w1 = ffn_w_in[lyr].astype(BF16)
    return _tail(yf, yr, bon, g, o_na, x, mod[2], mod[3], mod[4], mod[5],
                 rw_ln_g[lyr][None, :], rw_ln_b[lyr][None, :], ones_bd, wo[:NA_WIDTH], wo[NA_WIDTH:],
                 norm2_g[lyr][None, :], w1[:, :d_ff], w1[:, d_ff:], ffn_w_out[lyr].astype(BF16))
```

```python
import functools

import numpy as np
import jax
import jax.numpy as jnp
from jax import lax
from jax.experimental import pallas as pl
from jax.experimental.pallas import tpu as pltpu

F32 = jnp.float32
BF16 = jnp.bfloat16

GRID_W = 64
NA_HEADS = 8
HEAD_DIM = 64
NA_WIDTH = NA_HEADS * HEAD_DIM
NA_WIN_ROWS = 8
NA_WIN_COLS = 16
RW_HEADS = 8
RW_WIDTH = RW_HEADS * HEAD_DIM
DECAY_LORA = 64
AAA_LORA = 64
GATE_LORA = 128
NORM_EPS = 1e-6
RW_GN_EPS = 64e-5

LANE = 128
RW_COLS = 3 * RW_WIDTH + 3 * LANE
OFF_WD = 3 * RW_WIDTH
OFF_AD = OFF_WD + LANE
OFF_GD = OFF_AD + LANE
CHUNK = 64
INV_BASE = 16
NA_ROWS_PER_STEP = 4
NEG_BIG = -1e30
VMEM_LIMIT = 56 * 1024 * 1024


def _dot(a, b):
    return jnp.dot(a, b, preferred_element_type=F32)


def _bdot(a, b):
    return jnp.dot(a.astype(BF16), b.astype(BF16), preferred_element_type=F32)


def _bdot_nt(a, b):
    return lax.dot_general(a.astype(BF16), b.astype(BF16), (((1,), (1,)), ((), ())),
                           preferred_element_type=F32)


def _bdot_tn(a, b):
    return lax.dot_general(a.astype(BF16), b.astype(BF16), (((0,), (0,)), ((), ())),
                           preferred_element_type=F32)


def _split2(a):
    hi = a.astype(BF16)
    lo = (a - hi.astype(F32)).astype(BF16)
    return hi, lo


def _split3(a):
    hi = a.astype(BF16)
    r1 = a - hi.astype(F32)
    mid = r1.astype(BF16)
    lo = (r1 - mid.astype(F32)).astype(BF16)
    return hi, mid, lo


def _dot_hl(a, w):
    hi, lo = _split2(a)
    return _dot(hi, w) + _dot(lo, w)


def _dot_hl3(a, w_hi, w_lo):
    hi, lo = _split2(a)
    return _dot(hi, w_hi) + (_dot(lo, w_hi) + _dot(hi, w_lo))


def _sigmoid(x):
    return jax.nn.sigmoid(x)


def _const_spec(shape):
    nd = len(shape)
    return pl.BlockSpec(shape, lambda *_: (0,) * nd)


def _ada_kernel(c_ref, w_ref, b_ref, o_ref):
    c = c_ref[...]
    s = c * _sigmoid(c)
    w_hi, w_lo = _split2(w_ref[...])
    o_ref[...] = _dot_hl3(s, w_hi, w_lo) + b_ref[...]


def _ada(c_rows, w_ada, b_ada):
    rows, d = c_rows.shape
    n = w_ada.shape[1]
    tn = 1024
    return pl.pallas_call(
        _ada_kernel,
        grid=(n // tn,),
        in_specs=[pl.BlockSpec((rows, d), lambda j: (0, 0)),
                  pl.BlockSpec((d, tn), lambda j: (0, j)),
                  pl.BlockSpec((1, tn), lambda j: (0, j))],
        out_specs=pl.BlockSpec((rows, tn), lambda j: (0, j)),
        out_shape=jax.ShapeDtypeStruct((rows, n), F32),
        compiler_params=pltpu.CompilerParams(dimension_semantics=("arbitrary",), vmem_limit_bytes=VMEM_LIMIT),
        name="ada",
    )(c_rows, w_ada, b_ada)


def _inproj_kernel(x_ref, sh_ref, sc_ref, g_ref, wna_ref, wrw_ref, qg_ref, kg_ref, ones_ref,
                   q_ref, k_ref, v_ref, prw_ref):
    x = x_ref[0]
    ms = jnp.mean(x * x, axis=-1, keepdims=True)
    y = x * lax.rsqrt(ms + NORM_EPS) * g_ref[...]
    u = (y * (1.0 + sc_ref[0]) + sh_ref[0]).astype(BF16)
    pn = _dot(u, wna_ref[...])

    def head_norm(t, gain):
        ss = _dot_hl(t * t, ones_ref[...]) * (1.0 / HEAD_DIM)
        return t * lax.rsqrt(ss + NORM_EPS) * gain

    q = head_norm(pn[:, :NA_WIDTH], qg_ref[...]) * (HEAD_DIM ** -0.5)
    k = head_norm(pn[:, NA_WIDTH:2 * NA_WIDTH], kg_ref[...])
    q_ref[0] = q.astype(BF16)
    k_ref[0] = k.astype(BF16)
    v_ref[0] = pn[:, 2 * NA_WIDTH:].astype(BF16)
    prw_ref[0] = _dot(u, wrw_ref[...])


def _inproj(x, shift, scale, g1, w_na, w_rw, qg, kg, ones_bd):
    b, t, d = x.shape
    tm = min(512, t)
    vec = pl.BlockSpec((1, 1, d), lambda bi, i: (bi, 0, 0))
    tok = lambda w: pl.BlockSpec((1, tm, w), lambda bi, i: (bi, i, 0))
    return pl.pallas_call(
        _inproj_kernel,
        grid=(b, t // tm),
        in_specs=[tok(d), vec, vec, _const_spec((1, d)), _const_spec(w_na.shape), _const_spec(w_rw.shape),
                  _const_spec((1, NA_WIDTH)), _const_spec((1, NA_WIDTH)), _const_spec(ones_bd.shape)],
        out_specs=[tok(NA_WIDTH), tok(NA_WIDTH), tok(NA_WIDTH), tok(RW_COLS)],
        out_shape=[jax.ShapeDtypeStruct((b, t, NA_WIDTH), BF16)] * 3
        + [jax.ShapeDtypeStruct((b, t, RW_COLS), F32)],
        compiler_params=pltpu.CompilerParams(dimension_semantics=("arbitrary", "arbitrary"),
                                             vmem_limit_bytes=VMEM_LIMIT),
        name="inproj",
    )(x, shift, scale, g1, w_na, w_rw, qg, kg, ones_bd)


def _na_bias_tables(rpb, rows, r_blk):
    kr_n = r_blk + NA_WIN_ROWS - 1
    nblk = rows // r_blk
    n_h, n_di, n_dj = rpb.shape
    w = GRID_W
    lo = w - NA_WIN_COLS
    strip = jnp.pad(rpb, ((0, 0), (0, 0), (lo, 2 * w - 1 - n_dj - lo)))
    skew = jnp.tile(strip, (1, 1, w + 1))[:, :, :2 * w * w].reshape(n_h, n_di, w, 2 * w)[..., :w]
    toe = skew[:, :, ::-1, :]
    j = np.arange(w)
    cs = np.clip(j - NA_WIN_COLS // 2, 0, w - NA_WIN_COLS)
    c = np.arange(w)
    cvalid = (c[None, :] >= cs[:, None]) & (c[None, :] < cs[:, None] + NA_WIN_COLS)
    toe = jnp.where(cvalid[None, None], toe, NEG_BIG)
    neg = jnp.full((n_h, w, w), NEG_BIG, F32)
    tabs = []
    for ib in (0, min(1, nblk - 1), nblk - 1):
        i0 = ib * r_blk
        base = int(np.clip(i0 - NA_WIN_ROWS // 2, 0, rows - kr_n))
        q_rows = []
        for i in range(i0, i0 + r_blk):
            rs = int(np.clip(i - NA_WIN_ROWS // 2, 0, rows - NA_WIN_ROWS))
            blocks = []
            for kr in range(base, base + kr_n):
                blocks.append(toe[:, kr - i + NA_WIN_ROWS - 1] if rs <= kr < rs + NA_WIN_ROWS else neg)
            q_rows.append(jnp.concatenate(blocks, axis=-1))
        tabs.append(jnp.concatenate(q_rows, axis=1))
    return jnp.stack(tabs)


def _na_kernel(q_ref, k_ref, v_ref, kc_ref, vc_ref, bias_ref, o_ref, *, r_blk, rows):
    i = pl.program_id(2)
    kr_n = r_blk + NA_WIN_ROWS - 1
    base = jnp.clip(i * r_blk - NA_WIN_ROWS // 2, 0, rows - kr_n)
    start = pl.multiple_of(base * GRID_W, GRID_W)
    q = q_ref[0]
    kw = k_ref[0, pl.ds(start, kr_n * GRID_W), :]
    vw = v_ref[0, pl.ds(start, kr_n * GRID_W), :]
    kc = kc_ref[0]
    vc = vc_ref[0]
    lane = lax.broadcasted_iota(jnp.int32, q.shape, 1)
    outs = []
    for hh in range(2):
        sel = (lane < HEAD_DIM) if hh == 0 else (lane >= HEAD_DIM)
        qh = jnp.where(sel, q, jnp.zeros_like(q))
        s = _bdot_nt(qh, kw) + bias_ref[0, hh]
        sc = _bdot_nt(qh, kc)
        m = jnp.maximum(jnp.max(s, axis=-1, keepdims=True), jnp.max(sc, axis=-1, keepdims=True))
        p = jnp.exp(s - m)
        pc = jnp.exp(sc - m)
        l = jnp.sum(p, axis=-1, keepdims=True) + jnp.sum(pc, axis=-1, keepdims=True)
        o = _bdot(p, vw) + _bdot(pc, vc)
        outs.append(o / l)
    o_ref[0] = jnp.where(lane < HEAD_DIM, outs[0], outs[1]).astype(o_ref.dtype)


def _na(q, k, v, kc, vc, bias):
    b, t, _ = q.shape
    l_ctx = kc.shape[1]
    rows = t // GRID_W
    r_blk = NA_ROWS_PER_STEP
    kr_n = r_blk + NA_WIN_ROWS - 1
    nblk = rows // r_blk
    assert rows % r_blk == 0 and rows >= kr_n and t % GRID_W == 0
    n_pair = NA_WIDTH // LANE

    def bias_map(bi, hp, i):
        case = jnp.where(i == 0, 0, jnp.where(i == nblk - 1, 2, 1))
        return (case, hp, 0, 0)

    return pl.pallas_call(
        functools.partial(_na_kernel, r_blk=r_blk, rows=rows),
        grid=(b, n_pair, nblk),
        in_specs=[pl.BlockSpec((1, r_blk * GRID_W, LANE), lambda bi, hp, i: (bi, i, hp)),
                  pl.BlockSpec((1, t, LANE), lambda bi, hp, i: (bi, 0, hp)),
                  pl.BlockSpec((1, t, LANE), lambda bi, hp, i: (bi, 0, hp)),
                  pl.BlockSpec((1, l_ctx, LANE), lambda bi, hp, i: (bi, 0, hp)),
                  pl.BlockSpec((1, l_ctx, LANE), lambda bi, hp, i: (bi, 0, hp)),
                  pl.BlockSpec((1, 2, r_blk * GRID_W, kr_n * GRID_W), bias_map)],
        out_specs=pl.BlockSpec((1, r_blk * GRID_W, LANE), lambda bi, hp, i: (bi, i, hp)),
        out_shape=jax.ShapeDtypeStruct((b, t, NA_WIDTH), BF16),
        compiler_params=pltpu.CompilerParams(dimension_semantics=("arbitrary",) * 3, vmem_limit_bytes=VMEM_LIMIT),
        name="na",
    )(q, k, v, kc, vc, bias)


def _softplus(z):
    return jnp.maximum(z, 0.0) + jnp.log(1.0 + jnp.exp(-jnp.abs(z)))


def _rw_pre_kernel(p_ref, pp_ref, pn_ref, mup_ref, mun_ref, w0_ref, wup_hi_ref, wup_lo_ref, a0_ref, aup_hi_ref,
                   aup_lo_ref, gup_hi_ref, gup_lo_ref, kkw_ref, ka_ref, rk_ref, ones_ref, tri_ref,
                   g_ref, bon_ref, rh_ref, y0_ref, m_ref, n_ref):
    c = pl.program_id(1)
    nc = pl.num_programs(1)
    cl = CHUNK
    p = p_ref[0]
    row = lax.broadcasted_iota(jnp.int32, (cl, 1), 0)
    halo_prev = pp_ref[0, 7:8, :] * (c > 0).astype(F32)
    halo_next = pn_ref[0, 0:1, :] * (c < nc - 1).astype(F32)
    prev = jnp.where(row == 0, halo_prev, pltpu.roll(p, 1, 0))
    nxt = jnp.where(row == cl - 1, halo_next, pltpu.roll(p, cl - 1, 0))
    ps = p + mup_ref[...] * (prev - p) + mun_ref[...] * (nxt - p)

    r = ps[:, 0:RW_WIDTH]
    k = ps[:, RW_WIDTH:2 * RW_WIDTH]
    v = ps[:, 2 * RW_WIDTH:3 * RW_WIDTH]
    wd = ps[:, OFF_WD:OFF_WD + LANE]
    ad = ps[:, OFF_AD:OFF_AD + LANE]
    gd = ps[:, OFF_GD:OFF_GD + LANE]

    lora_w = _dot_hl3(jnp.tanh(wd), wup_hi_ref[...], wup_lo_ref[...])
    lora_a = _dot_hl3(ad, aup_hi_ref[...], aup_lo_ref[...])
    g_ref[0] = _dot_hl3(_sigmoid(gd), gup_hi_ref[...], gup_lo_ref[...])

    ones = ones_ref[...]
    kkv = k * kkw_ref[...]
    kk = kkv / jnp.maximum(jnp.sqrt(_dot_hl(kkv * kkv, ones)), 1e-12)

    ri = lax.broadcasted_iota(jnp.int32, (cl, cl), 0)
    ci = lax.broadcasted_iota(jnp.int32, (cl, cl), 1)
    eye = ri == ci
    eye_f = eye.astype(F32)
    n_sq = int(np.log2(INV_BASE)) - 1
    blk = []
    size = INV_BASE
    while size <= cl:
        sh = int(np.log2(size))
        blk.append((ri >> sh) == (ci >> sh))
        size *= 2

    bonus = jnp.zeros((cl, RW_WIDTH), F32)
    per_dir = []
    for d in range(2):
        sl_d = slice(d * RW_WIDTH, (d + 1) * RW_WIDTH)
        z = w0_ref[d:d + 1, :] + lora_w[:, sl_d]
        lw = -jnp.exp(-_softplus(-z) - 0.5)
        a = _sigmoid(a0_ref[d:d + 1, :] + lora_a[:, sl_d])
        kd = k * (1.0 + (a - 1.0) * ka_ref[...])
        bvec = kk * a
        bonus = bonus + _dot_hl(r * kd * rk_ref[...], ones) * v

        l_hi, l_mid, l_lo = _split3(lw)
        tri = tri_ref[d]
        cum = _dot(tri, l_hi) + (_dot(tri, l_mid) + _dot(tri, l_lo))
        tot = jnp.sum(lw, axis=0, keepdims=True)
        e_neg = jnp.exp(-cum)
        e_tot = jnp.exp(tot - cum)
        per_dir.append(dict(at=(-kk * jnp.exp(cum - lw)).astype(BF16), rt=r * jnp.exp(cum),
                            bt=(bvec * e_neg).astype(BF16), kt=(kd * e_neg).astype(BF16),
                            bb=(bvec * e_tot).astype(BF16), kb=(kd * e_tot).astype(BF16), p_tot=jnp.exp(tot)))
    bon_ref[0] = bonus
    v_bf = v.astype(BF16)

    probs = [(d, h) for d in range(2) for h in range(RW_HEADS)]
    hsl = lambda h: slice(h * HEAD_DIM, (h + 1) * HEAD_DIM)
    strict = ((ci < ri), (ci > ri))
    incl = ((ci <= ri), (ci >= ri))
    at = [per_dir[d]["at"][:, hsl(h)] for d, h in probs]
    rt = [per_dir[d]["rt"][:, hsl(h)] for d, h in probs]
    bt = [per_dir[d]["bt"][:, hsl(h)] for d, h in probs]
    kt = [per_dir[d]["kt"][:, hsl(h)] for d, h in probs]
    vh = [v_bf[:, hsl(h)] for d, h in probs]
    n_p = len(probs)
    rng = range(n_p)
    aab = [jnp.where(strict[probs[i][0]], _bdot_nt(at[i], bt[i]), 0.0) for i in rng]
    aak = [jnp.where(strict[probs[i][0]], _bdot_nt(at[i], kt[i]), 0.0).astype(BF16) for i in rng]
    rt_bf = [rt[i].astype(BF16) for i in rng]
    arb = [jnp.where(incl[probs[i][0]], _bdot_nt(rt_bf[i], bt[i]), 0.0).astype(BF16) for i in rng]
    ark = [jnp.where(incl[probs[i][0]], _bdot_nt(rt_bf[i], kt[i]), 0.0).astype(BF16) for i in rng]
    apow = [jnp.where(blk[0], aab[i], 0.0) for i in rng]
    tinv = [apow[i] + eye_f for i in rng]
    for _ in range(n_sq):
        apow = [_bdot(apow[i], apow[i]) for i in rng]
        tinv = [tinv[i] + _bdot(tinv[i], apow[i]) for i in rng]
    for lvl in range(1, len(blk)):
        sel = jnp.logical_and(blk[lvl], jnp.logical_not(blk[lvl - 1]))
        tmp = [_bdot(jnp.where(sel, aab[i], 0.0), tinv[i]) for i in rng]
        tinv = [tinv[i] + _bdot(tinv[i], tmp[i]) for i in rng]
    tinv = [tinv[i].astype(BF16) for i in rng]
    ah = [_bdot(tinv[i], at[i]).astype(BF16) for i in rng]
    av = [_bdot(aak[i], vh[i]) for i in rng]
    wh = [_bdot(tinv[i], av[i]).astype(BF16) for i in rng]
    for i, (d, h) in enumerate(probs):
        rh_ref[d, 0, :, hsl(h)] = rt[i] + _bdot(arb[i], ah[i])
    for i, (d, h) in enumerate(probs):
        y0_ref[d, 0, :, hsl(h)] = _bdot(arb[i], wh[i]) + _bdot(ark[i], vh[i])
    for i, (d, h) in enumerate(probs):
        bb = per_dir[d]["bb"][:, hsl(h)]
        m_ref[d, 0, 0, h] = jnp.where(eye, per_dir[d]["p_tot"][:, hsl(h)], 0.0) + _bdot_tn(bb, ah[i])
    for i, (d, h) in enumerate(probs):
        bb = per_dir[d]["bb"][:, hsl(h)]
        kb = per_dir[d]["kb"][:, hsl(h)]
        n_ref[d, 0, 0, h] = _bdot_tn(bb, wh[i]) + _bdot_tn(kb, vh[i])


def _rw_pre(prw, consts):
    b, t, _ = prw.shape
    cl = CHUNK
    nc = t // cl
    assert t % cl == 0 and cl % 8 == 0
    sub = cl // 8
    nb8 = t // 8
    tok = lambda w: pl.BlockSpec((1, cl, w), lambda bi, c: (bi, c, 0))
    in_specs = [tok(RW_COLS),
                pl.BlockSpec((1, 8, RW_COLS), lambda bi, c: (bi, jnp.maximum(c * sub - 1, 0), 0)),
                pl.BlockSpec((1, 8, RW_COLS), lambda bi, c: (bi, jnp.minimum((c + 1) * sub, nb8 - 1), 0))]
    in_specs += [_const_spec(a.shape) for a in consts]
    dirtok = pl.BlockSpec((2, 1, cl, RW_WIDTH), lambda bi, c: (0, bi, c, 0))
    mat = pl.BlockSpec((2, 1, 1, RW_HEADS, HEAD_DIM, HEAD_DIM), lambda bi, c: (0, bi, c, 0, 0, 0))
    return pl.pallas_call(
        _rw_pre_kernel,
        grid=(b, nc),
        in_specs=in_specs,
        out_specs=[tok(RW_WIDTH), tok(RW_WIDTH), dirtok, dirtok, mat, mat],
        out_shape=[jax.ShapeDtypeStruct((b, t, RW_WIDTH), F32)] * 2
        + [jax.ShapeDtypeStruct((2, b, t, RW_WIDTH), F32)] * 2
        + [jax.ShapeDtypeStruct((2, b, nc, RW_HEADS, HEAD_DIM, HEAD_DIM), F32)] * 2,
        compiler_params=pltpu.CompilerParams(dimension_semantics=("arbitrary", "arbitrary"),
                                             vmem_limit_bytes=VMEM_LIMIT),
        name="rw_pre",
    )(prw, prw, prw, *consts)


def _rw_scan_kernel(rhf_ref, y0f_ref, mf_ref, nf_ref, rhr_ref, y0r_ref, mr_ref, nr_ref, z0_ref,
                    yf_ref, yr_ref, zf_ref, z_scr):
    j = pl.program_id(0)

    @pl.when(j == 0)
    def _():
        z_scr[...] = z0_ref[...]

    nb = z_scr.shape[1]
    dirs = ((rhf_ref, y0f_ref, mf_ref, nf_ref, yf_ref), (rhr_ref, y0r_ref, mr_ref, nr_ref, yr_ref))
    for d, (rh_ref, y0_ref, m_ref, n_ref, y_ref) in enumerate(dirs):
        for bi in range(nb):
            for h in range(RW_HEADS):
                sl = slice(h * HEAD_DIM, (h + 1) * HEAD_DIM)
                z = z_scr[d, bi, h]
                y_ref[bi, :, sl] = _bdot(rh_ref[0, bi, :, sl], z) + y0_ref[0, bi, :, sl]
                z_scr[d, bi, h] = _bdot(m_ref[0, bi, 0, h], z) + n_ref[0, bi, 0, h]

    @pl.when(j == pl.num_programs(0) - 1)
    def _():
        zf_ref[...] = z_scr[...]


def _rw_scan(rh, y0, m, n, z0):
    _, b, t, _ = rh.shape
    cl = CHUNK
    nc = t // cl
    tokf = pl.BlockSpec((1, b, cl, RW_WIDTH), lambda j: (0, 0, j, 0))
    tokr = pl.BlockSpec((1, b, cl, RW_WIDTH), lambda j: (1, 0, nc - 1 - j, 0))
    matf = pl.BlockSpec((1, b, 1, RW_HEADS, HEAD_DIM, HEAD_DIM), lambda j: (0, 0, j, 0, 0, 0))
    matr = pl.BlockSpec((1, b, 1, RW_HEADS, HEAD_DIM, HEAD_DIM), lambda j: (1, 0, nc - 1 - j, 0, 0, 0))
    zspec = _const_spec(z0.shape)
    return pl.pallas_call(
        _rw_scan_kernel,
        grid=(nc,),
        in_specs=[tokf, tokf, matf, matf, tokr, tokr, matr, matr, zspec],
        out_specs=[pl.BlockSpec((b, cl, RW_WIDTH), lambda j: (0, j, 0)),
                   pl.BlockSpec((b, cl, RW_WIDTH), lambda j: (0, nc - 1 - j, 0)),
                   zspec],
        out_shape=[jax.ShapeDtypeStruct((b, t, RW_WIDTH), F32)] * 2 + [jax.ShapeDtypeStruct(z0.shape, F32)],
        scratch_shapes=[pltpu.VMEM(z0.shape, F32)],
        compiler_params=pltpu.CompilerParams(dimension_semantics=("arbitrary",), vmem_limit_bytes=VMEM_LIMIT),
        name="rw_scan",
    )(rh, y0, m, n, rh, y0, m, n, z0)


def _tail_kernel(yf_ref, yr_ref, bon_ref, g_ref, ona_ref, x_ref, gate1_ref, sh_ref, sc_ref, gate2_ref,
                 lng_ref, lnb_ref, ones_ref, wona_ref, worw_ref, g2_ref, w1g_ref, w1u_ref, w2_ref, o_ref, *, ff_chunk):
    ones = ones_ref[...]
    wkv = yf_ref[0] + yr_ref[0]
    mu = _dot_hl(wkv, ones) * (1.0 / HEAD_DIM)
    dlt = wkv - mu
    var = _dot_hl(dlt * dlt, ones) * (1.0 / HEAD_DIM)
    yn = dlt * lax.rsqrt(var + RW_GN_EPS) * lng_ref[...] + lnb_ref[...]
    orw = ((yn + bon_ref[0]) * g_ref[0]).astype(BF16)
    mix = _dot(ona_ref[0], wona_ref[...]) + _dot(orw, worw_ref[...])
    h1 = x_ref[0] + gate1_ref[0] * mix
    ms = jnp.mean(h1 * h1, axis=-1, keepdims=True)
    u = (h1 * lax.rsqrt(ms + NORM_EPS) * g2_ref[...] * (1.0 + sc_ref[0]) + sh_ref[0]).astype(BF16)
    d_ff = w2_ref.shape[0]
    acc = jnp.zeros(h1.shape, F32)
    for ci in range(d_ff // ff_chunk):
        sl = slice(ci * ff_chunk, (ci + 1) * ff_chunk)
        gt = _dot(u, w1g_ref[:, sl])
        up = _dot(u, w1u_ref[:, sl])
        act = (gt * _sigmoid(gt) * up).astype(BF16)
        acc = acc + _dot(act, w2_ref[sl, :])
    o_ref[0] = h1 + gate2_ref[0] * acc


def _ff_chunk(d_ff):
    best = LANE
    for cand in range(LANE, d_ff + 1, LANE):
        if d_ff % cand == 0 and cand <= 1536:
            best = cand
    return best


def _tail(yf, yr, bon, g, ona, x, gate1, sh, sc, gate2, lng, lnb, ones_bd, wona, worw, g2, w1g, w1u, w2):
    b, t, d = x.shape
    tm = min(256, t)
    vec = pl.BlockSpec((1, 1, d), lambda bi, i: (bi, 0, 0))
    tok = lambda w: pl.BlockSpec((1, tm, w), lambda bi, i: (bi, i, 0))
    consts = (lng, lnb, ones_bd, wona, worw, g2, w1g, w1u, w2)
    return pl.pallas_call(
        functools.partial(_tail_kernel, ff_chunk=_ff_chunk(w2.shape[0])),
        grid=(b, t // tm),
        in_specs=[tok(RW_WIDTH)] * 4 + [tok(NA_WIDTH), tok(d), vec, vec, vec, vec]
        + [_const_spec(a.shape) for a in consts],
        out_specs=tok(d),
        out_shape=jax.ShapeDtypeStruct((b, t, d), F32),
        compiler_params=pltpu.CompilerParams(dimension_semantics=("arbitrary", "arbitrary"),
                                             vmem_limit_bytes=VMEM_LIMIT),
        name="tail",
    )(yf, yr, bon, g, ona, x, gate1, sh, sc, gate2, *consts)


def _pad_cols(w, width):
    return jnp.pad(w, ((0, 0), (0, width - w.shape[1])))


def _rw_layout(w):
    o = 3 * RW_WIDTH
    wd = w[:, o:o + 2 * DECAY_LORA]
    ad = _pad_cols(w[:, o + 2 * DECAY_LORA:o + 2 * DECAY_LORA + AAA_LORA], LANE)
    gd = w[:, o + 2 * DECAY_LORA + AAA_LORA:]
    return jnp.concatenate([w[:, :o], wd, ad, gd], axis=1)


def _hl(w):
    hi = w.astype(BF16)
    return hi, (w - hi.astype(F32)).astype(BF16)


def kernel(x, c, ctx, c_ctx, norm1_g, norm2_g, w_ada, b_ada, w_in, na_q_g, na_k_g, na_rpb, rw_mu_prev, rw_mu_next,
           rw_w0, rw_w_up, rw_a0, rw_a_up, rw_g_up, rw_k_k, rw_k_a, rw_r_k, rw_ln_g, rw_ln_b, w_out, ffn_w_in,
           ffn_w_out):
    depth = w_in.shape[0]
    assert depth == 1, "single-layer kernel"
    b, t, d = x.shape
    l_ctx = ctx.shape[1]
    lyr = 0

    n_rows = -(-(b + 1) // 8) * 8
    c_rows = jnp.zeros((n_rows, d), F32).at[:b].set(c).at[b].set(c_ctx)
    mod_all = _ada(c_rows, w_ada[lyr], b_ada[lyr][None, :])
    mod = [mod_all[:b, i * d:(i + 1) * d][:, None, :] for i in range(6)]
    modc = [jnp.broadcast_to(mod_all[b, i * d:(i + 1) * d][None, None, :], (b, 1, d)) for i in range(2)]

    w_na = w_in[lyr][:, :3 * NA_WIDTH].astype(BF16)
    w_rw = _rw_layout(w_in[lyr][:, 3 * NA_WIDTH:]).astype(BF16)
    hd = np.arange(NA_WIDTH) // HEAD_DIM
    ones_bd = jnp.asarray(hd[:, None] == hd[None, :], BF16)
    qg = jnp.tile(na_q_g[lyr], NA_HEADS)[None, :]
    kg = jnp.tile(na_k_g[lyr], NA_HEADS)[None, :]
    g1 = norm1_g[lyr][None, :]

    q, k, v, prw = _inproj(x, mod[0], mod[1], g1, w_na, w_rw, qg, kg, ones_bd)
    _, kc, vc, prw_c = _inproj(ctx, modc[0], modc[1], g1, w_na, w_rw, qg, kg, ones_bd)

    bias = _na_bias_tables(na_rpb[lyr], t // GRID_W, NA_ROWS_PER_STEP)
    o_na = _na(q, k, v, kc, vc, bias)

    mu_p = _rw_layout(rw_mu_prev[lyr][None, :])
    mu_n = _rw_layout(rw_mu_next[lyr][None, :])
    zeros_up = jnp.zeros((DECAY_LORA, RW_WIDTH), F32)
    wup = jnp.concatenate([jnp.concatenate([rw_w_up[lyr, 0], zeros_up], axis=1),
                           jnp.concatenate([zeros_up, rw_w_up[lyr, 1]], axis=1)], axis=0)
    aup = jnp.concatenate([jnp.concatenate([rw_a_up[lyr, 0], rw_a_up[lyr, 1]], axis=1),
                           jnp.zeros((LANE - AAA_LORA, 2 * RW_WIDTH), F32)], axis=0)
    ri = np.arange(CHUNK)
    tri = jnp.asarray(np.stack([ri[None, :] <= ri[:, None], ri[None, :] >= ri[:, None]]), BF16)
    consts = (mu_p, mu_n, rw_w0[lyr], *_hl(wup), rw_a0[lyr], *_hl(aup), *_hl(rw_g_up[lyr]),
              rw_k_k[lyr][None, :], rw_k_a[lyr][None, :], rw_r_k[lyr].reshape(1, RW_WIDTH), ones_bd, tri)

    _, _, rh_c, y0_c, m_c, n_c = _rw_pre(prw_c, consts)
    g, bon, rh, y0, m, n = _rw_pre(prw, consts)
    z0 = jnp.zeros((2, b, RW_HEADS, HEAD_DIM, HEAD_DIM), F32)
    _, _, z_ctx = _rw_scan(rh_c, y0_c, m_c, n_c, z0)
    yf, yr, _ = _rw_scan(rh, y0, m, n, z_ctx)

    wo = w_out[lyr].astype(BF16)
    d_ff = ffn_w_out.shape[1]
    w1 = ffn_w_in[lyr].astype(BF16)
    return _tail(yf, yr, bon, g, o_na, x, mod[2], mod[3], mod[4], mod[5],
                 rw_ln_g[lyr][None, :], rw_ln_b[lyr][None, :], ones_bd, wo[:NA_WIDTH], wo[NA_WIDTH:],
                 norm2_g[lyr][None, :], w1[:, :d_ff], w1[:, d_ff:], ffn_w_out[lyr].astype(BF16))
```

```python
import functools

import numpy as np
import jax
import jax.numpy as jnp
from jax import lax
from jax.experimental import pallas as pl
from jax.experimental.pallas import tpu as pltpu

F32 = jnp.float32
BF16 = jnp.bfloat16

GRID_W = 64
NA_HEADS = 8
HEAD_DIM = 64
NA_WIDTH = NA_HEADS * HEAD_DIM
NA_WIN_ROWS = 8
NA_WIN_COLS = 16
RW_HEADS = 8
RW_WIDTH = RW_HEADS * HEAD_DIM
DECAY_LORA = 64
AAA_LORA = 64
GATE_LORA = 128
NORM_EPS = 1e-6
RW_GN_EPS = 64e-5

LANE = 128
RW_COLS = 3 * RW_WIDTH + 3 * LANE
OFF_WD = 3 * RW_WIDTH
OFF_AD = OFF_WD + LANE
OFF_GD = OFF_AD + LANE
CHUNK = 64
INV_BASE = 16
RW_CHUNKS_PER_STEP = 4
NA_ROWS_PER_STEP = 4
NEG_BIG = -1e30
VMEM_LIMIT = 56 * 1024 * 1024


def _dot(a, b):
    return jnp.dot(a, b, preferred_element_type=F32)


def _bdot(a, b):
    return jnp.dot(a.astype(BF16), b.astype(BF16), preferred_element_type=F32)


def _bdot_nt(a, b):
    return lax.dot_general(a.astype(BF16), b.astype(BF16), (((1,), (1,)), ((), ())),
                           preferred_element_type=F32)


def _bdot_tn(a, b):
    return lax.dot_general(a.astype(BF16), b.astype(BF16), (((0,), (0,)), ((), ())),
                           preferred_element_type=F32)


def _split2(a):
    hi = a.astype(BF16)
    lo = (a - hi.astype(F32)).astype(BF16)
    return hi, lo


def _split3(a):
    hi = a.astype(BF16)
    r1 = a - hi.astype(F32)
    mid = r1.astype(BF16)
    lo = (r1 - mid.astype(F32)).astype(BF16)
    return hi, mid, lo


def _head_sum(x, ones_grp):
    m, w = x.shape
    grp = ones_grp.shape[0]
    hi, lo = _split2(x)
    parts = [t[:, g * grp:(g + 1) * grp] for t in (hi, lo) for g in range(w // grp)]
    res = _dot(jnp.concatenate(parts, axis=0), ones_grp)
    n = w // grp
    return jnp.concatenate([res[g * m:(g + 1) * m] + res[(n + g) * m:(n + g + 1) * m] for g in range(n)], axis=1)


def _dot_hl3(a, w_hi, w_lo):
    m = a.shape[0]
    hi, lo = _split2(a)
    res = _dot(jnp.concatenate([hi, lo], axis=0), w_hi)
    return res[:m] + (res[m:] + _dot(hi, w_lo))


def _sigmoid(x):
    return jax.nn.sigmoid(x)


def _const_spec(shape, single=False):
    nd = len(shape)
    if single:
        return pl.BlockSpec(shape, lambda *_: (0,) * nd, pipeline_mode=pl.Buffered(1))
    return pl.BlockSpec(shape, lambda *_: (0,) * nd)


def _ada_kernel(c_ref, w_ref, b_ref, o_ref):
    c = c_ref[...]
    s = c * _sigmoid(c)
    w_hi, w_lo = _split2(w_ref[...])
    o_ref[...] = _dot_hl3(s, w_hi, w_lo) + b_ref[...]


def _ada(c_rows, w_ada, b_ada):
    rows, d = c_rows.shape
    n = w_ada.shape[1]
    tn = 1024
    return pl.pallas_call(
        _ada_kernel,
        grid=(n // tn,),
        in_specs=[pl.BlockSpec((rows, d), lambda j: (0, 0)),
                  pl.BlockSpec((d, tn), lambda j: (0, j)),
                  pl.BlockSpec((1, tn), lambda j: (0, j))],
        out_specs=pl.BlockSpec((rows, tn), lambda j: (0, j)),
        out_shape=jax.ShapeDtypeStruct((rows, n), F32),
        compiler_params=pltpu.CompilerParams(dimension_semantics=("arbitrary",), vmem_limit_bytes=VMEM_LIMIT),
        name="ada",
    )(c_rows, w_ada, b_ada)


def _inproj_kernel(x_ref, sh_ref, sc_ref, g_ref, wna_ref, wrw_ref, qg_ref, kg_ref, ones_ref,
                   q_ref, k_ref, v_ref, prw_ref):
    x = x_ref[0]
    ms = jnp.mean(x * x, axis=-1, keepdims=True)
    y = x * lax.rsqrt(ms + NORM_EPS) * g_ref[...]
    u = (y * (1.0 + sc_ref[0]) + sh_ref[0]).astype(BF16)
    pn = _dot(u, wna_ref[...])

    def head_norm(t, gain):
        ss = _head_sum(t * t, ones_ref[...]) * (1.0 / HEAD_DIM)
        return t * lax.rsqrt(ss + NORM_EPS) * gain

    q = head_norm(pn[:, :NA_WIDTH], qg_ref[...]) * (HEAD_DIM ** -0.5)
    k = head_norm(pn[:, NA_WIDTH:2 * NA_WIDTH], kg_ref[...])
    q_ref[0] = q.astype(BF16)
    k_ref[0] = k.astype(BF16)
    v_ref[0] = pn[:, 2 * NA_WIDTH:].astype(BF16)
    prw_ref[0] = _dot(u, wrw_ref[...])


def _inproj(x, shift, scale, g1, w_na, w_rw, qg, kg, ones_bd):
    b, t, d = x.shape
    tm = min(512, t)
    vec = pl.BlockSpec((1, 1, d), lambda bi, i: (bi, 0, 0))
    tok = lambda w: pl.BlockSpec((1, tm, w), lambda bi, i: (bi, i, 0))
    return pl.pallas_call(
        _inproj_kernel,
        grid=(b, t // tm),
        in_specs=[tok(d), vec, vec, _const_spec((1, d)), _const_spec(w_na.shape), _const_spec(w_rw.shape),
                  _const_spec((1, NA_WIDTH)), _const_spec((1, NA_WIDTH)), _const_spec(ones_bd.shape)],
        out_specs=[tok(NA_WIDTH), tok(NA_WIDTH), tok(NA_WIDTH), tok(RW_COLS)],
        out_shape=[jax.ShapeDtypeStruct((b, t, NA_WIDTH), BF16)] * 3
        + [jax.ShapeDtypeStruct((b, t, RW_COLS), F32)],
        compiler_params=pltpu.CompilerParams(dimension_semantics=("arbitrary", "arbitrary"),
                                             vmem_limit_bytes=VMEM_LIMIT),
        name="inproj",
    )(x, shift, scale, g1, w_na, w_rw, qg, kg, ones_bd)


def _na_bias_tables(rpb, rows, r_blk):
    kr_n = r_blk + NA_WIN_ROWS - 1
    nblk = rows // r_blk
    n_h, n_di, n_dj = rpb.shape
    w = GRID_W
    lo = w - NA_WIN_COLS
    strip = jnp.pad(rpb, ((0, 0), (0, 0), (lo, 2 * w - 1 - n_dj - lo)))
    skew = jnp.tile(strip, (1, 1, w + 1))[:, :, :2 * w * w].reshape(n_h, n_di, w, 2 * w)[..., :w]
    toe = skew[:, :, ::-1, :]
    j = np.arange(w)
    cs = np.clip(j - NA_WIN_COLS // 2, 0, w - NA_WIN_COLS)
    c = np.arange(w)
    cvalid = (c[None, :] >= cs[:, None]) & (c[None, :] < cs[:, None] + NA_WIN_COLS)
    toe = jnp.where(cvalid[None, None], toe, NEG_BIG)
    neg = jnp.full((n_h, w, w), NEG_BIG, F32)
    tabs = []
    for ib in (0, min(1, nblk - 1), nblk - 1):
        i0 = ib * r_blk
        base = int(np.clip(i0 - NA_WIN_ROWS // 2, 0, rows - kr_n))
        q_rows = []
        for i in range(i0, i0 + r_blk):
            rs = int(np.clip(i - NA_WIN_ROWS // 2, 0, rows - NA_WIN_ROWS))
            blocks = []
            for kr in range(base, base + kr_n):
                blocks.append(toe[:, kr - i + NA_WIN_ROWS - 1] if rs <= kr < rs + NA_WIN_ROWS else neg)
            q_rows.append(jnp.concatenate(blocks, axis=-1))
        tabs.append(jnp.concatenate(q_rows, axis=1))
    return jnp.stack(tabs)


def _na_kernel(q_ref, k_ref, v_ref, kc_ref, vc_ref, bias_ref, o_ref, *, r_blk, rows):
    i = pl.program_id(2)
    kr_n = r_blk + NA_WIN_ROWS - 1
    base = jnp.clip(i * r_blk - NA_WIN_ROWS // 2, 0, rows - kr_n)
    start = pl.multiple_of(base * GRID_W, GRID_W)
    q = q_ref[0]
    kw = k_ref[0, pl.ds(start, kr_n * GRID_W), :]
    vw = v_ref[0, pl.ds(start, kr_n * GRID_W), :]
    kc = kc_ref[0]
    vc = vc_ref[0]
    lane = lax.broadcasted_iota(jnp.int32, q.shape, 1)
    qh = [jnp.where(sel, q, jnp.zeros_like(q)) for sel in (lane < HEAD_DIM, lane >= HEAD_DIM)]
    s = [_bdot_nt(qh[hh], kw) + bias_ref[0, hh] for hh in range(2)]
    sc = [_bdot_nt(qh[hh], kc) for hh in range(2)]
    outs = []
    for hh in range(2):
        m = jnp.maximum(jnp.max(s[hh], axis=-1, keepdims=True), jnp.max(sc[hh], axis=-1, keepdims=True))
        p = jnp.exp(s[hh] - m)
        pc = jnp.exp(sc[hh] - m)
        l = jnp.sum(p, axis=-1, keepdims=True) + jnp.sum(pc, axis=-1, keepdims=True)
        o = _bdot(p, vw) + _bdot(pc, vc)
        outs.append(o / l)
    o_ref[0] = jnp.where(lane < HEAD_DIM, outs[0], outs[1]).astype(o_ref.dtype)


def _na(q, k, v, kc, vc, bias):
    b, t, _ = q.shape
    l_ctx = kc.shape[1]
    rows = t // GRID_W
    r_blk = NA_ROWS_PER_STEP
    kr_n = r_blk + NA_WIN_ROWS - 1
    nblk = rows // r_blk
    assert rows % r_blk == 0 and rows >= kr_n and t % GRID_W == 0
    n_pair = NA_WIDTH // LANE

    def bias_map(bi, hp, i):
        case = jnp.where(i == 0, 0, jnp.where(i == nblk - 1, 2, 1))
        return (case, hp, 0, 0)

    return pl.pallas_call(
        functools.partial(_na_kernel, r_blk=r_blk, rows=rows),
        grid=(b, n_pair, nblk),
        in_specs=[pl.BlockSpec((1, r_blk * GRID_W, LANE), lambda bi, hp, i: (bi, i, hp)),
                  pl.BlockSpec((1, t, LANE), lambda bi, hp, i: (bi, 0, hp)),
                  pl.BlockSpec((1, t, LANE), lambda bi, hp, i: (bi, 0, hp)),
                  pl.BlockSpec((1, l_ctx, LANE), lambda bi, hp, i: (bi, 0, hp)),
                  pl.BlockSpec((1, l_ctx, LANE), lambda bi, hp, i: (bi, 0, hp)),
                  pl.BlockSpec((1, 2, r_blk * GRID_W, kr_n * GRID_W), bias_map)],
        out_specs=pl.BlockSpec((1, r_blk * GRID_W, LANE), lambda bi, hp, i: (bi, i, hp)),
        out_shape=jax.ShapeDtypeStruct((b, t, NA_WIDTH), BF16),
        compiler_params=pltpu.CompilerParams(dimension_semantics=("arbitrary",) * 3, vmem_limit_bytes=VMEM_LIMIT),
        name="na",
    )(q, k, v, kc, vc, bias)


def _softplus(z):
    return jnp.maximum(z, 0.0) + jnp.log(1.0 + jnp.exp(-jnp.abs(z)))


def _rw_pre_kernel(p_ref, pp_ref, pn_ref, mup_ref, mun_ref, w0_ref, wup_hi_ref, wup_lo_ref, a0_ref, aup_hi_ref,
                   aup_lo_ref, gup_hi_ref, gup_lo_ref, kkw_ref, ka_ref, rk_ref, tri_ref, bd_ref,
                   g_ref, bon_ref, rh_ref, y0_ref, m_ref, n_ref):
    c = pl.program_id(1)
    nc = pl.num_programs(1)
    cl = CHUNK
    p = p_ref[0]
    tm = p.shape[0]
    row = lax.broadcasted_iota(jnp.int32, (tm, 1), 0)
    halo_prev = pp_ref[0, 7:8, :] * (c > 0).astype(F32)
    halo_next = pn_ref[0, 0:1, :] * (c < nc - 1).astype(F32)
    prev = jnp.where(row == 0, halo_prev, pltpu.roll(p, 1, 0))
    nxt = jnp.where(row == tm - 1, halo_next, pltpu.roll(p, tm - 1, 0))
    ps = p + mup_ref[...] * (prev - p) + mun_ref[...] * (nxt - p)

    r = ps[:, 0:RW_WIDTH]
    k = ps[:, RW_WIDTH:2 * RW_WIDTH]
    v = ps[:, 2 * RW_WIDTH:3 * RW_WIDTH]
    wd = ps[:, OFF_WD:OFF_WD + LANE]
    ad = ps[:, OFF_AD:OFF_AD + LANE]
    gd = ps[:, OFF_GD:OFF_GD + LANE]

    lora_w = _dot_hl3(jnp.tanh(wd), wup_hi_ref[...], wup_lo_ref[...])
    lora_a = _dot_hl3(ad, aup_hi_ref[...], aup_lo_ref[...])
    g_ref[0] = _dot_hl3(_sigmoid(gd), gup_hi_ref[...], gup_lo_ref[...])

    bd_mask = bd_ref[...]
    kkv = k * kkw_ref[...]
    kk = kkv / jnp.maximum(jnp.sqrt(_head_sum(kkv * kkv, bd_mask)), 1e-12)

    grp = 4 * HEAD_DIM
    n_grp = RW_WIDTH // grp
    ri = lax.broadcasted_iota(jnp.int32, (cl, grp), 0)
    ci = lax.broadcasted_iota(jnp.int32, (cl, grp), 1) & (HEAD_DIM - 1)
    eye_f = (ri == ci).astype(F32)
    n_sq = int(np.log2(INV_BASE)) - 1
    blk = []
    size = INV_BASE
    while size <= cl:
        sh = int(np.log2(size))
        blk.append((ri >> sh) == (ci >> sh))
        size *= 2
    strict = ((ci < ri), (ci > ri))
    incl = ((ci <= ri), (ci >= ri))

    def bd(x):
        xb = x.astype(BF16)
        return jnp.concatenate([xb] * (grp // cl), axis=0) * bd_mask

    n_ch = tm // cl
    csl = lambda ch: slice(ch * cl, (ch + 1) * cl)
    bonus = jnp.zeros((tm, RW_WIDTH), F32)
    per_dir = []
    for d in range(2):
        sl_d = slice(d * RW_WIDTH, (d + 1) * RW_WIDTH)
        z = w0_ref[d:d + 1, :] + lora_w[:, sl_d]
        lw = -jnp.exp(-_softplus(-z) - 0.5)
        a = _sigmoid(a0_ref[d:d + 1, :] + lora_a[:, sl_d])
        kd = k * (1.0 + (a - 1.0) * ka_ref[...])
        bvec = kk * a
        bonus = bonus + _head_sum(r * kd * rk_ref[...], bd_mask) * v

        l_hi, l_mid, l_lo = _split3(lw)
        tri = tri_ref[d]
        cum = _dot(tri, l_hi) + (_dot(tri, l_mid) + _dot(tri, l_lo))
        tot = [jnp.sum(lw[csl(ch)], axis=0, keepdims=True) for ch in range(n_ch)]
        tot_rows = jnp.concatenate([jnp.broadcast_to(tc, (cl, RW_WIDTH)) for tc in tot], axis=0)
        e_neg = jnp.exp(-cum)
        e_tot = jnp.exp(tot_rows - cum)
        per_dir.append(dict(at=(-kk * jnp.exp(cum - lw)).astype(BF16), rt=r * jnp.exp(cum),
                            bt=(bvec * e_neg).astype(BF16), kt=(kd * e_neg).astype(BF16),
                            bb=(bvec * e_tot).astype(BF16), kb=(kd * e_tot).astype(BF16),
                            p_tot=[jnp.exp(tc) for tc in tot]))
    bon_ref[0] = bonus
    v_bf = v.astype(BF16)

    probs = [(ch, d, g) for ch in range(n_ch) for d in range(2) for g in range(n_grp)]
    rng = range(len(probs))
    gsl = lambda g: slice(g * grp, (g + 1) * grp)
    at = [per_dir[d]["at"][csl(ch), gsl(g)] for ch, d, g in probs]
    rt = [per_dir[d]["rt"][csl(ch), gsl(g)] for ch, d, g in probs]
    bd_v = {(ch, g): bd(v_bf[csl(ch), gsl(g)]) for ch in range(n_ch) for g in range(n_grp)}
    lhs = [jnp.concatenate([at[i], rt[i].astype(BF16)], axis=0) for i in rng]
    s_b = [_bdot_nt(lhs[i], bd(per_dir[d]["bt"][csl(ch), gsl(g)])) for i, (ch, d, g) in enumerate(probs)]
    s_k = [_bdot_nt(lhs[i], bd(per_dir[d]["kt"][csl(ch), gsl(g)])) for i, (ch, d, g) in enumerate(probs)]
    aab = [jnp.where(strict[probs[i][1]], s_b[i][:cl], 0.0) for i in rng]
    arb = [jnp.where(incl[probs[i][1]], s_b[i][cl:], 0.0).astype(BF16) for i in rng]
    aak = [jnp.where(strict[probs[i][1]], s_k[i][:cl], 0.0).astype(BF16) for i in rng]
    ark = [jnp.where(incl[probs[i][1]], s_k[i][cl:], 0.0).astype(BF16) for i in rng]
    d0 = [jnp.where(blk[0], aab[i], 0.0) for i in rng]
    tinv = [d0[i] + eye_f for i in rng]
    apow = [_bdot(d0[i], bd(d0[i])) for i in rng]
    for _ in range(n_sq - 1):
        res = [_bdot(jnp.concatenate([tinv[i], apow[i]], axis=0), bd(apow[i])) for i in rng]
        tinv = [tinv[i] + res[i][:cl] for i in rng]
        apow = [res[i][cl:] for i in rng]
    tinv = [tinv[i] + _bdot(tinv[i], bd(apow[i])) for i in rng]
    for lvl in range(1, len(blk)):
        sel = jnp.logical_and(blk[lvl], jnp.logical_not(blk[lvl - 1]))
        tmp = [_bdot(jnp.where(sel, aab[i], 0.0), bd(tinv[i])) for i in rng]
        tinv = [tinv[i] + _bdot(tinv[i], bd(tmp[i])) for i in rng]
    tinv = [tinv[i].astype(BF16) for i in rng]
    ah = [_bdot(tinv[i], bd(at[i])).astype(BF16) for i in rng]
    av = [_bdot(aak[i], bd_v[probs[i][0], probs[i][2]]) for i in rng]
    wh = [_bdot(tinv[i], bd(av[i])).astype(BF16) for i in rng]
    for i, (ch, d, g) in enumerate(probs):
        rh_ref[d, 0, csl(ch), gsl(g)] = (rt[i] + _bdot(arb[i], bd(ah[i]))).astype(rh_ref.dtype)
    for i, (ch, d, g) in enumerate(probs):
        y0_ref[d, 0, csl(ch), gsl(g)] = _bdot(arb[i], bd(wh[i])) + _bdot(ark[i], bd_v[ch, g])
    hpg = grp // HEAD_DIM
    hsl = lambda j: slice(j * HEAD_DIM, (j + 1) * HEAD_DIM)
    eye_h = eye_f[:, :HEAD_DIM] > 0.5
    heads = [(i, ch, d, g, j) for i, (ch, d, g) in enumerate(probs) for j in range(hpg)]
    for i, ch, d, g, j in heads:
        h = g * hpg + j
        bb = per_dir[d]["bb"][csl(ch), hsl(h)]
        m_ref[d, 0, ch, h] = (jnp.where(eye_h, per_dir[d]["p_tot"][ch][:, hsl(h)], 0.0)
                              + _bdot_tn(bb, ah[i][:, hsl(j)])).astype(m_ref.dtype)
    for i, ch, d, g, j in heads:
        h = g * hpg + j
        bb = per_dir[d]["bb"][csl(ch), hsl(h)]
        kb = per_dir[d]["kb"][csl(ch), hsl(h)]
        n_ref[d, 0, ch, h] = _bdot_tn(bb, wh[i][:, hsl(j)]) + _bdot_tn(kb, v_bf[csl(ch), hsl(h)])


def _rw_pre(prw, consts):
    b, t, _ = prw.shape
    cl = CHUNK
    cps = RW_CHUNKS_PER_STEP
    tm = cps * cl
    nc = t // cl
    assert t % tm == 0 and tm % 8 == 0
    sub = tm // 8
    nb8 = t // 8
    tok = lambda w: pl.BlockSpec((1, tm, w), lambda bi, c: (bi, c, 0))
    in_specs = [tok(RW_COLS),
                pl.BlockSpec((1, 8, RW_COLS), lambda bi, c: (bi, jnp.maximum(c * sub - 1, 0), 0)),
                pl.BlockSpec((1, 8, RW_COLS), lambda bi, c: (bi, jnp.minimum((c + 1) * sub, nb8 - 1), 0))]
    in_specs += [_const_spec(a.shape) for a in consts]
    dirtok = pl.BlockSpec((2, 1, tm, RW_WIDTH), lambda bi, c: (0, bi, c, 0))
    mat = pl.BlockSpec((2, 1, cps, RW_HEADS, HEAD_DIM, HEAD_DIM), lambda bi, c: (0, bi, c, 0, 0, 0))
    return pl.pallas_call(
        _rw_pre_kernel,
        grid=(b, t // tm),
        in_specs=in_specs,
        out_specs=[tok(RW_WIDTH), tok(RW_WIDTH), dirtok, dirtok, mat, mat],
        out_shape=[jax.ShapeDtypeStruct((b, t, RW_WIDTH), F32)] * 2
        + [jax.ShapeDtypeStruct((2, b, t, RW_WIDTH), dt) for dt in (BF16, F32)]
        + [jax.ShapeDtypeStruct((2, b, nc, RW_HEADS, HEAD_DIM, HEAD_DIM), dt) for dt in (BF16, F32)],
        compiler_params=pltpu.CompilerParams(dimension_semantics=("arbitrary", "arbitrary"),
                                             vmem_limit_bytes=VMEM_LIMIT),
        name="rw_pre",
    )(prw, prw, prw, *consts)


def _rw_scan_kernel(rhf_ref, y0f_ref, mf_ref, nf_ref, rhr_ref, y0r_ref, mr_ref, nr_ref, z0_ref,
                    yf_ref, yr_ref, zf_ref, z_scr):
    j = pl.program_id(0)

    @pl.when(j == 0)
    def _():
        z_scr[...] = z0_ref[...]

    nb = z_scr.shape[1]
    dirs = ((rhf_ref, y0f_ref, mf_ref, nf_ref, yf_ref), (rhr_ref, y0r_ref, mr_ref, nr_ref, yr_ref))
    for d, (rh_ref, y0_ref, m_ref, n_ref, y_ref) in enumerate(dirs):
        for bi in range(nb):
            for h in range(RW_HEADS):
                sl = slice(h * HEAD_DIM, (h + 1) * HEAD_DIM)
                z = z_scr[d, bi, h]
                y_ref[bi, :, sl] = _bdot(rh_ref[0, bi, :, sl], z) + y0_ref[0, bi, :, sl]
                z_scr[d, bi, h] = _bdot(m_ref[0, bi, 0, h], z) + n_ref[0, bi, 0, h]

    @pl.when(j == pl.num_programs(0) - 1)
    def _():
        zf_ref[...] = z_scr[...]


def _rw_scan(rh, y0, m, n, z0):
    _, b, t, _ = rh.shape
    cl = CHUNK
    nc = t // cl
    tokf = pl.BlockSpec((1, b, cl, RW_WIDTH), lambda j: (0, 0, j, 0))
    tokr = pl.BlockSpec((1, b, cl, RW_WIDTH), lambda j: (1, 0, nc - 1 - j, 0))
    matf = pl.BlockSpec((1, b, 1, RW_HEADS, HEAD_DIM, HEAD_DIM), lambda j: (0, 0, j, 0, 0, 0))
    matr = pl.BlockSpec((1, b, 1, RW_HEADS, HEAD_DIM, HEAD_DIM), lambda j: (1, 0, nc - 1 - j, 0, 0, 0))
    zspec = _const_spec(z0.shape)
    return pl.pallas_call(
        _rw_scan_kernel,
        grid=(nc,),
        in_specs=[tokf, tokf, matf, matf, tokr, tokr, matr, matr, zspec],
        out_specs=[pl.BlockSpec((b, cl, RW_WIDTH), lambda j: (0, j, 0)),
                   pl.BlockSpec((b, cl, RW_WIDTH), lambda j: (0, nc - 1 - j, 0)),
                   zspec],
        out_shape=[jax.ShapeDtypeStruct((b, t, RW_WIDTH), F32)] * 2 + [jax.ShapeDtypeStruct(z0.shape, F32)],
        scratch_shapes=[pltpu.VMEM(z0.shape, F32)],
        compiler_params=pltpu.CompilerParams(dimension_semantics=("arbitrary",), vmem_limit_bytes=VMEM_LIMIT),
        name="rw_scan",
    )(rh, y0, m, n, rh, y0, m, n, z0)


def _tail_kernel(yf_ref, yr_ref, bon_ref, g_ref, ona_ref, x_ref, gate1_ref, sh_ref, sc_ref, gate2_ref,
                 lng_ref, lnb_ref, ones_ref, wona_ref, worw_ref, g2_ref, w1g_ref, w1u_ref, w2_ref, o_ref, *, ff_chunk):
    ones = ones_ref[...]
    wkv = yf_ref[0] + yr_ref[0]
    mu = _head_sum(wkv, ones) * (1.0 / HEAD_DIM)
    dlt = wkv - mu
    var = _head_sum(dlt * dlt, ones) * (1.0 / HEAD_DIM)
    yn = dlt * lax.rsqrt(var + RW_GN_EPS) * lng_ref[...] + lnb_ref[...]
    orw = ((yn + bon_ref[0]) * g_ref[0]).astype(BF16)
    mix = _dot(ona_ref[0], wona_ref[...]) + _dot(orw, worw_ref[...])
    h1 = x_ref[0] + gate1_ref[0] * mix
    ms = jnp.mean(h1 * h1, axis=-1, keepdims=True)
    u = (h1 * lax.rsqrt(ms + NORM_EPS) * g2_ref[...] * (1.0 + sc_ref[0]) + sh_ref[0]).astype(BF16)
    bounds = _ff_bounds(w2_ref.shape[0], ff_chunk)

    def up_proj(lo, hi):
        return _dot(u, w1g_ref[:, lo:hi]), _dot(u, w1u_ref[:, lo:hi])

    acc = jnp.zeros(h1.shape, F32)
    nxt = up_proj(*bounds[0])
    for ci, (lo, hi) in enumerate(bounds):
        gt, up = nxt
        if ci + 1 < len(bounds):
            nxt = up_proj(*bounds[ci + 1])
        act = (gt * _sigmoid(gt) * up).astype(BF16)
        acc = acc + _dot(act, w2_ref[lo:hi, :])
    o_ref[0] = h1 + gate2_ref[0] * acc


FF_CHUNK = 512


def _ff_bounds(d_ff, ff_chunk):
    edges = list(range(0, d_ff, ff_chunk)) + [d_ff]
    return [(lo, hi) for lo, hi in zip(edges[:-1], edges[1:])]


def _tail(yf, yr, bon, g, ona, x, gate1, sh, sc, gate2, lng, lnb, ones_bd, wona, worw, g2, w1g, w1u, w2):
    b, t, d = x.shape
    tm = min(512, t)
    vec = pl.BlockSpec((1, 1, d), lambda bi, i: (bi, 0, 0))
    tok = lambda w: pl.BlockSpec((1, tm, w), lambda bi, i: (bi, i, 0))
    consts = (lng, lnb, ones_bd, wona, worw, g2, w1g, w1u, w2)
    return pl.pallas_call(
        functools.partial(_tail_kernel, ff_chunk=FF_CHUNK),
        grid=(b, t // tm),
        in_specs=[tok(RW_WIDTH)] * 4 + [tok(NA_WIDTH), tok(d), vec, vec, vec, vec]
        + [_const_spec(a.shape, single=True) for a in consts],
        out_specs=tok(d),
        out_shape=jax.ShapeDtypeStruct((b, t, d), F32),
        compiler_params=pltpu.CompilerParams(dimension_semantics=("arbitrary", "arbitrary"),
                                             vmem_limit_bytes=VMEM_LIMIT),
        name="tail",
    )(yf, yr, bon, g, ona, x, gate1, sh, sc, gate2, *consts)


def _pad_cols(w, width):
    return jnp.pad(w, ((0, 0), (0, width - w.shape[1])))


def _rw_layout(w):
    o = 3 * RW_WIDTH
    wd = w[:, o:o + 2 * DECAY_LORA]
    ad = _pad_cols(w[:, o + 2 * DECAY_LORA:o + 2 * DECAY_LORA + AAA_LORA], LANE)
    gd = w[:, o + 2 * DECAY_LORA + AAA_LORA:]
    return jnp.concatenate([w[:, :o], wd, ad, gd], axis=1)


def _hl(w):
    hi = w.astype(BF16)
    return hi, (w - hi.astype(F32)).astype(BF16)


def kernel(x, c, ctx, c_ctx, norm1_g, norm2_g, w_ada, b_ada, w_in, na_q_g, na_k_g, na_rpb, rw_mu_prev, rw_mu_next,
           rw_w0, rw_w_up, rw_a0, rw_a_up, rw_g_up, rw_k_k, rw_k_a, rw_r_k, rw_ln_g, rw_ln_b, w_out, ffn_w_in,
           ffn_w_out):
    depth = w_in.shape[0]
    assert depth == 1, "single-layer kernel"
    b, t, d = x.shape
    l_ctx = ctx.shape[1]
    lyr = 0

    n_rows = -(-(b + 1) // 8) * 8
    c_rows = jnp.zeros((n_rows, d), F32).at[:b].set(c).at[b].set(c_ctx)
    mod_all = _ada(c_rows, w_ada[lyr], b_ada[lyr][None, :])
    mod = [mod_all[:b, i * d:(i + 1) * d][:, None, :] for i in range(6)]
    modc = [jnp.broadcast_to(mod_all[b, i * d:(i + 1) * d][None, None, :], (b, 1, d)) for i in range(2)]

    w_na = w_in[lyr][:, :3 * NA_WIDTH].astype(BF16)
    w_rw = _rw_layout(w_in[lyr][:, 3 * NA_WIDTH:]).astype(BF16)
    hd = np.arange(4 * HEAD_DIM) // HEAD_DIM
    ones_bd = jnp.asarray(hd[:, None] == hd[None, :], BF16)
    qg = jnp.tile(na_q_g[lyr], NA_HEADS)[None, :]
    kg = jnp.tile(na_k_g[lyr], NA_HEADS)[None, :]
    g1 = norm1_g[lyr][None, :]

    q, k, v, prw = _inproj(x, mod[0], mod[1], g1, w_na, w_rw, qg, kg, ones_bd)
    _, kc, vc, prw_c = _inproj(ctx, modc[0], modc[1], g1, w_na, w_rw, qg, kg, ones_bd)

    bias = _na_bias_tables(na_rpb[lyr], t // GRID_W, NA_ROWS_PER_STEP)
    o_na = _na(q, k, v, kc, vc, bias)

    mu_p = _rw_layout(rw_mu_prev[lyr][None, :])
    mu_n = _rw_layout(rw_mu_next[lyr][None, :])
    zeros_up = jnp.zeros((DECAY_LORA, RW_WIDTH), F32)
    wup = jnp.concatenate([jnp.concatenate([rw_w_up[lyr, 0], zeros_up], axis=1),
                           jnp.concatenate([zeros_up, rw_w_up[lyr, 1]], axis=1)], axis=0)
    aup = jnp.concatenate([jnp.concatenate([rw_a_up[lyr, 0], rw_a_up[lyr, 1]], axis=1),
                           jnp.zeros((LANE - AAA_LORA, 2 * RW_WIDTH), F32)], axis=0)
    ri = np.arange(RW_CHUNKS_PER_STEP * CHUNK)
    same = (ri[None, :] // CHUNK) == (ri[:, None] // CHUNK)
    tri = jnp.asarray(np.stack([same & (ri[None, :] <= ri[:, None]), same & (ri[None, :] >= ri[:, None])]), BF16)
    consts = (mu_p, mu_n, rw_w0[lyr], *_hl(wup), rw_a0[lyr], *_hl(aup), *_hl(rw_g_up[lyr]),
              rw_k_k[lyr][None, :], rw_k_a[lyr][None, :], rw_r_k[lyr].reshape(1, RW_WIDTH), tri, ones_bd)

    _, _, rh_c, y0_c, m_c, n_c = _rw_pre(prw_c, consts)
    g, bon, rh, y0, m, n = _rw_pre(prw, consts)
    z0 = jnp.zeros((2, b, RW_HEADS, HEAD_DIM, HEAD_DIM), F32)
    _, _, z_ctx = _rw_scan(rh_c, y0_c, m_c, n_c, z0)
    yf, yr, _ = _rw_scan(rh, y0, m, n, z_ctx)

    wo = w_out[lyr].astype(BF16)
    d_ff = ffn_w_out.shape[1]
    w1 = ffn_w_in[lyr].astype(BF16)
    return _tail(yf, yr, bon, g, o_na, x, mod[2], mod[3], mod[4], mod[5],
                 rw_ln_g[lyr][None, :], rw_ln_b[lyr][None, :], ones_bd, wo[:NA_WIDTH], wo[NA_WIDTH:],
                 norm2_g[lyr][None, :], w1[:, :d_ff], w1[:, d_ff:], ffn_w_out[lyr].astype(BF16))
```

```python
import functools

import numpy as np
import jax
import jax.numpy as jnp
from jax import lax
from jax.experimental import pallas as pl
from jax.experimental.pallas import tpu as pltpu

F32 = jnp.float32
BF16 = jnp.bfloat16

GRID_W = 64
NA_HEADS = 8
HEAD_DIM = 64
NA_WIDTH = NA_HEADS * HEAD_DIM
NA_WIN_ROWS = 8
NA_WIN_COLS = 16
RW_HEADS = 8
RW_WIDTH = RW_HEADS * HEAD_DIM
DECAY_LORA = 64
AAA_LORA = 64
GATE_LORA = 128
NORM_EPS = 1e-6
RW_GN_EPS = 64e-5

LANE = 128
RW_COLS = 3 * RW_WIDTH + 3 * LANE
OFF_WD = 3 * RW_WIDTH
OFF_AD = OFF_WD + LANE
OFF_GD = OFF_AD + LANE
CHUNK = 64
INV_BASE = 16
RW_SCAN_CHUNKS_PER_STEP = 2
RW_CHUNKS_PER_STEP = 4
NA_ROWS_PER_STEP = 4
NA_HEADS_PER_STEP = 4
NEG_BIG = -1e30
VMEM_LIMIT = 56 * 1024 * 1024


def _dot(a, b):
    return jnp.dot(a, b, preferred_element_type=F32)


def _bdot(a, b):
    return jnp.dot(a.astype(BF16), b.astype(BF16), preferred_element_type=F32)


def _bdot_nt(a, b):
    return lax.dot_general(a.astype(BF16), b.astype(BF16), (((1,), (1,)), ((), ())),
                           preferred_element_type=F32)


def _bdot_tn(a, b):
    return lax.dot_general(a.astype(BF16), b.astype(BF16), (((0,), (0,)), ((), ())),
                           preferred_element_type=F32)


def _split2(a):
    hi = a.astype(BF16)
    lo = (a - hi.astype(F32)).astype(BF16)
    return hi, lo


def _split3(a):
    hi = a.astype(BF16)
    r1 = a - hi.astype(F32)
    mid = r1.astype(BF16)
    lo = (r1 - mid.astype(F32)).astype(BF16)
    return hi, mid, lo


def _head_sum(x, ones_grp):
    m, w = x.shape
    grp = ones_grp.shape[0]
    hi, lo = _split2(x)
    parts = [t[:, g * grp:(g + 1) * grp] for t in (hi, lo) for g in range(w // grp)]
    res = _dot(jnp.concatenate(parts, axis=0), ones_grp)
    n = w // grp
    return jnp.concatenate([res[g * m:(g + 1) * m] + res[(n + g) * m:(n + g + 1) * m] for g in range(n)], axis=1)


def _dot_hl3(a, w_hi, w_lo):
    m = a.shape[0]
    hi, lo = _split2(a)
    res = _dot(jnp.concatenate([hi, lo], axis=0), w_hi)
    return res[:m] + (res[m:] + _dot(hi, w_lo))


def _sigmoid(x):
    return jax.nn.sigmoid(x)


def _const_spec(shape, single=False):
    nd = len(shape)
    if single:
        return pl.BlockSpec(shape, lambda *_: (0,) * nd, pipeline_mode=pl.Buffered(1))
    return pl.BlockSpec(shape, lambda *_: (0,) * nd)


def _ada_kernel(c_ref, w_ref, b_ref, o_ref):
    c = c_ref[...]
    s = c * _sigmoid(c)
    w_hi, w_lo = _split2(w_ref[...])
    o_ref[...] = _dot_hl3(s, w_hi, w_lo) + b_ref[...]


def _ada(c_rows, w_ada, b_ada):
    rows, d = c_rows.shape
    n = w_ada.shape[1]
    tn = 1024
    return pl.pallas_call(
        _ada_kernel,
        grid=(n // tn,),
        in_specs=[pl.BlockSpec((rows, d), lambda j: (0, 0)),
                  pl.BlockSpec((d, tn), lambda j: (0, j)),
                  pl.BlockSpec((1, tn), lambda j: (0, j))],
        out_specs=pl.BlockSpec((rows, tn), lambda j: (0, j)),
        out_shape=jax.ShapeDtypeStruct((rows, n), F32),
        compiler_params=pltpu.CompilerParams(dimension_semantics=("arbitrary",), vmem_limit_bytes=VMEM_LIMIT),
        name="ada",
    )(c_rows, w_ada, b_ada)


def _inproj_kernel(x_ref, sh_ref, sc_ref, g_ref, wna_ref, wrw_ref, qg_ref, kg_ref, ones_ref,
                   q_ref, k_ref, v_ref, prw_ref):
    x = x_ref[0]
    ms = jnp.mean(x * x, axis=-1, keepdims=True)
    y = x * lax.rsqrt(ms + NORM_EPS) * g_ref[...]
    u = (y * (1.0 + sc_ref[0]) + sh_ref[0]).astype(BF16)
    pn = _dot(u, wna_ref[...])

    def head_norm(t, gain):
        ss = _head_sum(t * t, ones_ref[...]) * (1.0 / HEAD_DIM)
        return t * lax.rsqrt(ss + NORM_EPS) * gain

    q = head_norm(pn[:, :NA_WIDTH], qg_ref[...]) * (HEAD_DIM ** -0.5)
    k = head_norm(pn[:, NA_WIDTH:2 * NA_WIDTH], kg_ref[...])
    q_ref[0] = q.astype(BF16)
    k_ref[0] = k.astype(BF16)
    v_ref[0] = pn[:, 2 * NA_WIDTH:].astype(BF16)
    prw_ref[0] = _dot(u, wrw_ref[...])


def _inproj(x, shift, scale, g1, w_na, w_rw, qg, kg, ones_bd):
    b, t, d = x.shape
    tm = min(512, t)
    vec = pl.BlockSpec((1, 1, d), lambda bi, i: (bi, 0, 0))
    tok = lambda w: pl.BlockSpec((1, tm, w), lambda bi, i: (bi, i, 0))
    return pl.pallas_call(
        _inproj_kernel,
        grid=(b, t // tm),
        in_specs=[tok(d), vec, vec, _const_spec((1, d)), _const_spec(w_na.shape), _const_spec(w_rw.shape),
                  _const_spec((1, NA_WIDTH)), _const_spec((1, NA_WIDTH)), _const_spec(ones_bd.shape)],
        out_specs=[tok(NA_WIDTH), tok(NA_WIDTH), tok(NA_WIDTH), tok(RW_COLS)],
        out_shape=[jax.ShapeDtypeStruct((b, t, NA_WIDTH), BF16)] * 3
        + [jax.ShapeDtypeStruct((b, t, RW_COLS), F32)],
        compiler_params=pltpu.CompilerParams(dimension_semantics=("arbitrary", "arbitrary"),
                                             vmem_limit_bytes=VMEM_LIMIT),
        name="inproj",
    )(x, shift, scale, g1, w_na, w_rw, qg, kg, ones_bd)


def _na_bias_tables(rpb, rows, r_blk):
    kr_n = r_blk + NA_WIN_ROWS - 1
    nblk = rows // r_blk
    n_h, n_di, n_dj = rpb.shape
    w = GRID_W
    lo = w - NA_WIN_COLS
    strip = jnp.pad(rpb, ((0, 0), (0, 0), (lo, 2 * w - 1 - n_dj - lo)))
    skew = jnp.tile(strip, (1, 1, w + 1))[:, :, :2 * w * w].reshape(n_h, n_di, w, 2 * w)[..., :w]
    toe = skew[:, :, ::-1, :]
    j = np.arange(w)
    cs = np.clip(j - NA_WIN_COLS // 2, 0, w - NA_WIN_COLS)
    c = np.arange(w)
    cvalid = (c[None, :] >= cs[:, None]) & (c[None, :] < cs[:, None] + NA_WIN_COLS)
    toe = jnp.where(cvalid[None, None], toe, NEG_BIG)
    neg = jnp.full((n_h, w, w), NEG_BIG, F32)
    tabs = []
    for ib in (0, min(1, nblk - 1), nblk - 1):
        i0 = ib * r_blk
        base = int(np.clip(i0 - NA_WIN_ROWS // 2, 0, rows - kr_n))
        q_rows = []
        for i in range(i0, i0 + r_blk):
            rs = int(np.clip(i - NA_WIN_ROWS // 2, 0, rows - NA_WIN_ROWS))
            blocks = []
            for kr in range(base, base + kr_n):
                blocks.append(toe[:, kr - i + NA_WIN_ROWS - 1] if rs <= kr < rs + NA_WIN_ROWS else neg)
            q_rows.append(jnp.concatenate(blocks, axis=-1))
        tabs.append(jnp.concatenate(q_rows, axis=1))
    return jnp.stack(tabs)


def _na_kernel(q_ref, k_ref, v_ref, kc_ref, vc_ref, bias_ref, o_ref, *, r_blk, rows):
    i = pl.program_id(2)
    kr_n = r_blk + NA_WIN_ROWS - 1
    base = jnp.clip(i * r_blk - NA_WIN_ROWS // 2, 0, rows - kr_n)
    start = pl.multiple_of(base * GRID_W, GRID_W)
    n_pair = q_ref.shape[2] // LANE
    lane = lax.broadcasted_iota(jnp.int32, (q_ref.shape[1], LANE), 1)
    heads = [(pr, hh) for pr in range(n_pair) for hh in range(2)]
    psl = lambda pr: slice(pr * LANE, (pr + 1) * LANE)
    kw = [k_ref[0, pl.ds(start, kr_n * GRID_W), psl(pr)] for pr in range(n_pair)]
    vw = [v_ref[0, pl.ds(start, kr_n * GRID_W), psl(pr)] for pr in range(n_pair)]
    s, sc = [], []
    for pr, hh in heads:
        q = q_ref[0, :, psl(pr)]
        qh = jnp.where((lane < HEAD_DIM) if hh == 0 else (lane >= HEAD_DIM), q, jnp.zeros_like(q))
        s.append(_bdot_nt(qh, kw[pr]) + bias_ref[0, 2 * pr + hh])
        sc.append(_bdot_nt(qh, kc_ref[0, :, psl(pr)]))
    outs = []
    for n, (pr, hh) in enumerate(heads):
        m = jnp.maximum(jnp.max(s[n], axis=-1, keepdims=True), jnp.max(sc[n], axis=-1, keepdims=True))
        p = jnp.exp(s[n] - m)
        pc = jnp.exp(sc[n] - m)
        l = jnp.sum(p, axis=-1, keepdims=True) + jnp.sum(pc, axis=-1, keepdims=True)
        o = _bdot(p, vw[pr]) + _bdot(pc, vc_ref[0, :, psl(pr)])
        outs.append(o / l)
    for pr in range(n_pair):
        o_ref[0, :, psl(pr)] = jnp.where(lane < HEAD_DIM, outs[2 * pr], outs[2 * pr + 1]).astype(o_ref.dtype)


def _na(q, k, v, kc, vc, bias):
    b, t, _ = q.shape
    l_ctx = kc.shape[1]
    rows = t // GRID_W
    r_blk = NA_ROWS_PER_STEP
    kr_n = r_blk + NA_WIN_ROWS - 1
    nblk = rows // r_blk
    assert rows % r_blk == 0 and rows >= kr_n and t % GRID_W == 0
    w = NA_HEADS_PER_STEP * HEAD_DIM
    n_grp = NA_WIDTH // w

    def bias_map(bi, hg, i):
        case = jnp.where(i == 0, 0, jnp.where(i == nblk - 1, 2, 1))
        return (case, hg, 0, 0)

    whole = lambda rows_: pl.BlockSpec((1, rows_, w), lambda bi, hg, i: (bi, 0, hg), pipeline_mode=pl.Buffered(1))
    return pl.pallas_call(
        functools.partial(_na_kernel, r_blk=r_blk, rows=rows),
        grid=(b, n_grp, nblk),
        in_specs=[pl.BlockSpec((1, r_blk * GRID_W, w), lambda bi, hg, i: (bi, i, hg)),
                  whole(t), whole(t), whole(l_ctx), whole(l_ctx),
                  pl.BlockSpec((1, NA_HEADS_PER_STEP, r_blk * GRID_W, kr_n * GRID_W), bias_map)],
        out_specs=pl.BlockSpec((1, r_blk * GRID_W, w), lambda bi, hg, i: (bi, i, hg)),
        out_shape=jax.ShapeDtypeStruct((b, t, NA_WIDTH), BF16),
        compiler_params=pltpu.CompilerParams(dimension_semantics=("arbitrary",) * 3, vmem_limit_bytes=VMEM_LIMIT),
        name="na",
    )(q, k, v, kc, vc, bias)


def _softplus(z):
    return jnp.maximum(z, 0.0) + jnp.log(1.0 + jnp.exp(-jnp.abs(z)))


def _rw_pre_kernel(p_ref, pp_ref, pn_ref, mup_ref, mun_ref, w0_ref, wup_hi_ref, wup_lo_ref, a0_ref, aup_hi_ref,
                   aup_lo_ref, gup_hi_ref, gup_lo_ref, kkw_ref, ka_ref, rk_ref, tri_ref, bd_ref,
                   g_ref, bon_ref, rh_ref, y0_ref, m_ref, n_ref):
    c = pl.program_id(1)
    nc = pl.num_programs(1)
    cl = CHUNK
    p = p_ref[0]
    tm = p.shape[0]
    row = lax.broadcasted_iota(jnp.int32, (tm, 1), 0)
    halo_prev = pp_ref[0, 7:8, :] * (c > 0).astype(F32)
    halo_next = pn_ref[0, 0:1, :] * (c < nc - 1).astype(F32)
    prev = jnp.where(row == 0, halo_prev, pltpu.roll(p, 1, 0))
    nxt = jnp.where(row == tm - 1, halo_next, pltpu.roll(p, tm - 1, 0))
    ps = p + mup_ref[...] * (prev - p) + mun_ref[...] * (nxt - p)

    r = ps[:, 0:RW_WIDTH]
    k = ps[:, RW_WIDTH:2 * RW_WIDTH]
    v = ps[:, 2 * RW_WIDTH:3 * RW_WIDTH]
    wd = ps[:, OFF_WD:OFF_WD + LANE]
    ad = ps[:, OFF_AD:OFF_AD + LANE]
    gd = ps[:, OFF_GD:OFF_GD + LANE]

    lora_w = _dot_hl3(jnp.tanh(wd), wup_hi_ref[...], wup_lo_ref[...])
    lora_a = _dot_hl3(ad, aup_hi_ref[...], aup_lo_ref[...])
    g_ref[0] = _dot_hl3(_sigmoid(gd), gup_hi_ref[...], gup_lo_ref[...])

    bd_mask = bd_ref[...]
    kkv = k * kkw_ref[...]
    kk = kkv / jnp.maximum(jnp.sqrt(_head_sum(kkv * kkv, bd_mask)), 1e-12)

    grp = 4 * HEAD_DIM
    n_grp = RW_WIDTH // grp
    ri = lax.broadcasted_iota(jnp.int32, (cl, grp), 0)
    ci = lax.broadcasted_iota(jnp.int32, (cl, grp), 1) & (HEAD_DIM - 1)
    eye_f = (ri == ci).astype(F32)
    n_sq = int(np.log2(INV_BASE)) - 1
    blk = []
    size = INV_BASE
    while size <= cl:
        sh = int(np.log2(size))
        blk.append((ri >> sh) == (ci >> sh))
        size *= 2
    strict = ((ci < ri), (ci > ri))
    incl = ((ci <= ri), (ci >= ri))

    def bd(x):
        xb = x.astype(BF16)
        return jnp.concatenate([xb] * (grp // cl), axis=0) * bd_mask

    n_ch = tm // cl
    csl = lambda ch: slice(ch * cl, (ch + 1) * cl)
    bonus = jnp.zeros((tm, RW_WIDTH), F32)
    per_dir = []
    for d in range(2):
        sl_d = slice(d * RW_WIDTH, (d + 1) * RW_WIDTH)
        z = w0_ref[d:d + 1, :] + lora_w[:, sl_d]
        lw = -jnp.exp(-_softplus(-z) - 0.5)
        a = _sigmoid(a0_ref[d:d + 1, :] + lora_a[:, sl_d])
        kd = k * (1.0 + (a - 1.0) * ka_ref[...])
        bvec = kk * a
        bonus = bonus + _head_sum(r * kd * rk_ref[...], bd_mask) * v

        l_hi, l_mid, l_lo = _split3(lw)
        tri = tri_ref[d]
        cum = _dot(tri, l_hi) + (_dot(tri, l_mid) + _dot(tri, l_lo))
        tot = [jnp.sum(lw[csl(ch)], axis=0, keepdims=True) for ch in range(n_ch)]
        tot_rows = jnp.concatenate([jnp.broadcast_to(tc, (cl, RW_WIDTH)) for tc in tot], axis=0)
        e_neg = jnp.exp(-cum)
        e_tot = jnp.exp(tot_rows - cum)
        per_dir.append(dict(at=(-kk * jnp.exp(cum - lw)).astype(BF16), rt=r * jnp.exp(cum),
                            bt=(bvec * e_neg).astype(BF16), kt=(kd * e_neg).astype(BF16),
                            bb=(bvec * e_tot).astype(BF16), kb=(kd * e_tot).astype(BF16),
                            p_tot=[jnp.exp(tc) for tc in tot]))
    bon_ref[0] = bonus
    v_bf = v.astype(BF16)

    probs = [(ch, d, g) for ch in range(n_ch) for d in range(2) for g in range(n_grp)]
    rng = range(len(probs))
    gsl = lambda g: slice(g * grp, (g + 1) * grp)
    at = [per_dir[d]["at"][csl(ch), gsl(g)] for ch, d, g in probs]
    rt = [per_dir[d]["rt"][csl(ch), gsl(g)] for ch, d, g in probs]
    bd_v = {(ch, g): bd(v_bf[csl(ch), gsl(g)]) for ch in range(n_ch) for g in range(n_grp)}
    lhs = [jnp.concatenate([at[i], rt[i].astype(BF16)], axis=0) for i in rng]
    s_b = [_bdot_nt(lhs[i], bd(per_dir[d]["bt"][csl(ch), gsl(g)])) for i, (ch, d, g) in enumerate(probs)]
    s_k = [_bdot_nt(lhs[i], bd(per_dir[d]["kt"][csl(ch), gsl(g)])) for i, (ch, d, g) in enumerate(probs)]
    aab = [jnp.where(strict[probs[i][1]], s_b[i][:cl], 0.0) for i in rng]
    arb = [jnp.where(incl[probs[i][1]], s_b[i][cl:], 0.0).astype(BF16) for i in rng]
    aak = [jnp.where(strict[probs[i][1]], s_k[i][:cl], 0.0).astype(BF16) for i in rng]
    ark = [jnp.where(incl[probs[i][1]], s_k[i][cl:], 0.0).astype(BF16) for i in rng]
    d0 = [jnp.where(blk[0], aab[i], 0.0) for i in rng]
    tinv = [d0[i] + eye_f for i in rng]
    apow = [_bdot(d0[i], bd(d0[i])) for i in rng]
    for _ in range(n_sq - 1):
        res = [_bdot(jnp.concatenate([tinv[i], apow[i]], axis=0), bd(apow[i])) for i in rng]
        tinv = [tinv[i] + res[i][:cl] for i in rng]
        apow = [res[i][cl:] for i in rng]
    tinv = [tinv[i] + _bdot(tinv[i], bd(apow[i])) for i in rng]
    for lvl in range(1, len(blk)):
        sel = jnp.logical_and(blk[lvl], jnp.logical_not(blk[lvl - 1]))
        tmp = [_bdot(jnp.where(sel, aab[i], 0.0), bd(tinv[i])) for i in rng]
        tinv = [tinv[i] + _bdot(tinv[i], bd(tmp[i])) for i in rng]
    tinv = [tinv[i].astype(BF16) for i in rng]
    ah = [_bdot(tinv[i], bd(at[i])).astype(BF16) for i in rng]
    av = [_bdot(aak[i], bd_v[probs[i][0], probs[i][2]]) for i in rng]
    wh = [_bdot(tinv[i], bd(av[i])).astype(BF16) for i in rng]
    for i, (ch, d, g) in enumerate(probs):
        rh_ref[d, 0, csl(ch), gsl(g)] = (rt[i] + _bdot(arb[i], bd(ah[i]))).astype(rh_ref.dtype)
    for i, (ch, d, g) in enumerate(probs):
        y0_ref[d, 0, csl(ch), gsl(g)] = _bdot(arb[i], bd(wh[i])) + _bdot(ark[i], bd_v[ch, g])
    r2 = lax.broadcasted_iota(jnp.int32, (grp, grp), 0) >> int(np.log2(HEAD_DIM))
    c2 = lax.broadcasted_iota(jnp.int32, (grp, grp), 1) >> int(np.log2(HEAD_DIM))
    same_head = r2 == c2

    def fold(full):
        fm = jnp.where(same_head, full, 0.0)
        out = fm[0:HEAD_DIM]
        for j in range(1, grp // HEAD_DIM):
            out = out + fm[j * HEAD_DIM:(j + 1) * HEAD_DIM]
        return out

    for i, (ch, d, g) in enumerate(probs):
        bb = per_dir[d]["bb"][csl(ch), gsl(g)]
        diag = eye_f * per_dir[d]["p_tot"][ch][:, gsl(g)]
        m_ref[d, 0, ch, :, gsl(g)] = (diag + fold(_bdot_tn(bb, ah[i]))).astype(m_ref.dtype)
    for i, (ch, d, g) in enumerate(probs):
        bk = jnp.concatenate([per_dir[d]["bb"][csl(ch), gsl(g)], per_dir[d]["kb"][csl(ch), gsl(g)]], axis=0)
        wv = jnp.concatenate([wh[i], v_bf[csl(ch), gsl(g)]], axis=0)
        n_ref[d, 0, ch, :, gsl(g)] = fold(_bdot_tn(bk, wv))


def _rw_pre(prw, consts):
    b, t, _ = prw.shape
    cl = CHUNK
    cps = RW_CHUNKS_PER_STEP
    tm = cps * cl
    nc = t // cl
    assert t % tm == 0 and tm % 8 == 0
    sub = tm // 8
    nb8 = t // 8
    tok = lambda w: pl.BlockSpec((1, tm, w), lambda bi, c: (bi, c, 0))
    in_specs = [tok(RW_COLS),
                pl.BlockSpec((1, 8, RW_COLS), lambda bi, c: (bi, jnp.maximum(c * sub - 1, 0), 0)),
                pl.BlockSpec((1, 8, RW_COLS), lambda bi, c: (bi, jnp.minimum((c + 1) * sub, nb8 - 1), 0))]
    in_specs += [_const_spec(a.shape) for a in consts]
    dirtok = pl.BlockSpec((2, 1, tm, RW_WIDTH), lambda bi, c: (0, bi, c, 0))
    mat = pl.BlockSpec((2, 1, cps, HEAD_DIM, RW_WIDTH), lambda bi, c: (0, bi, c, 0, 0))
    return pl.pallas_call(
        _rw_pre_kernel,
        grid=(b, t // tm),
        in_specs=in_specs,
        out_specs=[tok(RW_WIDTH), tok(RW_WIDTH), dirtok, dirtok, mat, mat],
        out_shape=[jax.ShapeDtypeStruct((b, t, RW_WIDTH), F32)] * 2
        + [jax.ShapeDtypeStruct((2, b, t, RW_WIDTH), dt) for dt in (BF16, F32)]
        + [jax.ShapeDtypeStruct((2, b, nc, HEAD_DIM, RW_WIDTH), dt) for dt in (BF16, F32)],
        compiler_params=pltpu.CompilerParams(dimension_semantics=("arbitrary", "arbitrary"),
                                             vmem_limit_bytes=VMEM_LIMIT),
        name="rw_pre",
    )(prw, prw, prw, *consts)


def _rw_scan_kernel(rhf_ref, y0f_ref, mf_ref, nf_ref, rhr_ref, y0r_ref, mr_ref, nr_ref, z0_ref, bd_ref,
                    yf_ref, yr_ref, zf_ref, z_scr):
    j = pl.program_id(0)

    @pl.when(j == 0)
    def _():
        z_scr[...] = z0_ref[...]

    cl = CHUNK
    nb = z_scr.shape[1]
    n_ch = rhf_ref.shape[2] // cl
    bd_mask = bd_ref[...]
    grp = bd_mask.shape[0]
    dirs = ((rhf_ref, y0f_ref, mf_ref, nf_ref, yf_ref), (rhr_ref, y0r_ref, mr_ref, nr_ref, yr_ref))
    for step in range(n_ch):
        for d, (rh_ref, y0_ref, m_ref, n_ref, y_ref) in enumerate(dirs):
            ch = step if d == 0 else n_ch - 1 - step
            rows = slice(ch * cl, (ch + 1) * cl)
            for bi in range(nb):
                for g in range(RW_WIDTH // grp):
                    gs = slice(g * grp, (g + 1) * grp)
                    z_bd = jnp.concatenate([z_scr[d, bi, :, gs].astype(BF16)] * (grp // HEAD_DIM), axis=0) * bd_mask
                    lhs = jnp.concatenate([rh_ref[0, bi, rows, gs], m_ref[0, bi, ch, :, gs]], axis=0)
                    res = _dot(lhs, z_bd)
                    y_ref[bi, rows, gs] = res[:cl] + y0_ref[0, bi, rows, gs]
                    z_scr[d, bi, :, gs] = res[cl:] + n_ref[0, bi, ch, :, gs]

    @pl.when(j == pl.num_programs(0) - 1)
    def _():
        zf_ref[...] = z_scr[...]


def _rw_scan(rh, y0, m, n, z0, bd_mask):
    _, b, t, _ = rh.shape
    cl = CHUNK
    cps = RW_SCAN_CHUNKS_PER_STEP
    tm = cps * cl
    ns = t // tm
    assert t % tm == 0
    tokf = pl.BlockSpec((1, b, tm, RW_WIDTH), lambda j: (0, 0, j, 0))
    tokr = pl.BlockSpec((1, b, tm, RW_WIDTH), lambda j: (1, 0, ns - 1 - j, 0))
    matf = pl.BlockSpec((1, b, cps, HEAD_DIM, RW_WIDTH), lambda j: (0, 0, j, 0, 0))
    matr = pl.BlockSpec((1, b, cps, HEAD_DIM, RW_WIDTH), lambda j: (1, 0, ns - 1 - j, 0, 0))
    zspec = _const_spec(z0.shape)
    return pl.pallas_call(
        _rw_scan_kernel,
        grid=(ns,),
        in_specs=[tokf, tokf, matf, matf, tokr, tokr, matr, matr, zspec, _const_spec(bd_mask.shape)],
        out_specs=[pl.BlockSpec((b, tm, RW_WIDTH), lambda j: (0, j, 0)),
                   pl.BlockSpec((b, tm, RW_WIDTH), lambda j: (0, ns - 1 - j, 0)),
                   zspec],
        out_shape=[jax.ShapeDtypeStruct((b, t, RW_WIDTH), F32)] * 2 + [jax.ShapeDtypeStruct(z0.shape, F32)],
        scratch_shapes=[pltpu.VMEM(z0.shape, F32)],
        compiler_params=pltpu.CompilerParams(dimension_semantics=("arbitrary",), vmem_limit_bytes=VMEM_LIMIT),
        name="rw_scan",
    )(rh, y0, m, n, rh, y0, m, n, z0, bd_mask)


def _tail_kernel(yf_ref, yr_ref, bon_ref, g_ref, ona_ref, x_ref, gate1_ref, sh_ref, sc_ref, gate2_ref,
                 lng_ref, lnb_ref, ones_ref, wona_ref, worw_ref, g2_ref, w1g_ref, w1u_ref, w2_ref, o_ref, *, ff_chunk):
    ones = ones_ref[...]
    wkv = yf_ref[0] + yr_ref[0]
    mu = _head_sum(wkv, ones) * (1.0 / HEAD_DIM)
    dlt = wkv - mu
    var = _head_sum(dlt * dlt, ones) * (1.0 / HEAD_DIM)
    yn = dlt * lax.rsqrt(var + RW_GN_EPS) * lng_ref[...] + lnb_ref[...]
    orw = ((yn + bon_ref[0]) * g_ref[0]).astype(BF16)
    mix = _dot(ona_ref[0], wona_ref[...]) + _dot(orw, worw_ref[...])
    h1 = x_ref[0] + gate1_ref[0] * mix
    ms = jnp.mean(h1 * h1, axis=-1, keepdims=True)
    u = (h1 * lax.rsqrt(ms + NORM_EPS) * g2_ref[...] * (1.0 + sc_ref[0]) + sh_ref[0]).astype(BF16)
    bounds = _ff_bounds(w2_ref.shape[0], ff_chunk)

    def up_proj(lo, hi):
        return _dot(u, w1g_ref[:, lo:hi]), _dot(u, w1u_ref[:, lo:hi])

    acc = jnp.zeros(h1.shape, F32)
    nxt = up_proj(*bounds[0])
    for ci, (lo, hi) in enumerate(bounds):
        gt, up = nxt
        if ci + 1 < len(bounds):
            nxt = up_proj(*bounds[ci + 1])
        act = (gt * _sigmoid(gt) * up).astype(BF16)
        acc = acc + _dot(act, w2_ref[lo:hi, :])
    o_ref[0] = h1 + gate2_ref[0] * acc


FF_CHUNK = 512


def _ff_bounds(d_ff, ff_chunk):
    edges = list(range(0, d_ff, ff_chunk)) + [d_ff]
    return [(lo, hi) for lo, hi in zip(edges[:-1], edges[1:])]


def _tail(yf, yr, bon, g, ona, x, gate1, sh, sc, gate2, lng, lnb, ones_bd, wona, worw, g2, w1g, w1u, w2):
    b, t, d = x.shape
    tm = min(512, t)
    vec = pl.BlockSpec((1, 1, d), lambda bi, i: (bi, 0, 0))
    tok = lambda w: pl.BlockSpec((1, tm, w), lambda bi, i: (bi, i, 0))
    consts = (lng, lnb, ones_bd, wona, worw, g2, w1g, w1u, w2)
    return pl.pallas_call(
        functools.partial(_tail_kernel, ff_chunk=FF_CHUNK),
        grid=(b, t // tm),
        in_specs=[tok(RW_WIDTH)] * 4 + [tok(NA_WIDTH), tok(d), vec, vec, vec, vec]
        + [_const_spec(a.shape, single=True) for a in consts],
        out_specs=tok(d),
        out_shape=jax.ShapeDtypeStruct((b, t, d), F32),
        compiler_params=pltpu.CompilerParams(dimension_semantics=("arbitrary", "arbitrary"),
                                             vmem_limit_bytes=VMEM_LIMIT),
        name="tail",
    )(yf, yr, bon, g, ona, x, gate1, sh, sc, gate2, *consts)


def _pad_cols(w, width):
    return jnp.pad(w, ((0, 0), (0, width - w.shape[1])))


def _rw_layout(w):
    o = 3 * RW_WIDTH
    wd = w[:, o:o + 2 * DECAY_LORA]
    ad = _pad_cols(w[:, o + 2 * DECAY_LORA:o + 2 * DECAY_LORA + AAA_LORA], LANE)
    gd = w[:, o + 2 * DECAY_LORA + AAA_LORA:]
    return jnp.concatenate([w[:, :o], wd, ad, gd], axis=1)


def _hl(w):
    hi = w.astype(BF16)
    return hi, (w - hi.astype(F32)).astype(BF16)


def kernel(x, c, ctx, c_ctx, norm1_g, norm2_g, w_ada, b_ada, w_in, na_q_g, na_k_g, na_rpb, rw_mu_prev, rw_mu_next,
           rw_w0, rw_w_up, rw_a0, rw_a_up, rw_g_up, rw_k_k, rw_k_a, rw_r_k, rw_ln_g, rw_ln_b, w_out, ffn_w_in,
           ffn_w_out):
    depth = w_in.shape[0]
    assert depth == 1, "single-layer kernel"
    b, t, d = x.shape
    l_ctx = ctx.shape[1]
    lyr = 0

    n_rows = -(-(b + 1) // 8) * 8
    c_rows = jnp.zeros((n_rows, d), F32).at[:b].set(c).at[b].set(c_ctx)
    mod_all = _ada(c_rows, w_ada[lyr], b_ada[lyr][None, :])
    mod = [mod_all[:b, i * d:(i + 1) * d][:, None, :] for i in range(6)]
    modc = [jnp.broadcast_to(mod_all[b, i * d:(i + 1) * d][None, None, :], (b, 1, d)) for i in range(2)]

    w_na = w_in[lyr][:, :3 * NA_WIDTH].astype(BF16)
    w_rw = _rw_layout(w_in[lyr][:, 3 * NA_WIDTH:]).astype(BF16)
    hd = np.arange(4 * HEAD_DIM) // HEAD_DIM
    ones_bd = jnp.asarray(hd[:, None] == hd[None, :], BF16)
    qg = jnp.tile(na_q_g[lyr], NA_HEADS)[None, :]
    kg = jnp.tile(na_k_g[lyr], NA_HEADS)[None, :]
    g1 = norm1_g[lyr][None, :]

    q, k, v, prw = _inproj(x, mod[0], mod[1], g1, w_na, w_rw, qg, kg, ones_bd)
    _, kc, vc, prw_c = _inproj(ctx, modc[0], modc[1], g1, w_na, w_rw, qg, kg, ones_bd)

    bias = _na_bias_tables(na_rpb[lyr], t // GRID_W, NA_ROWS_PER_STEP)
    o_na = _na(q, k, v, kc, vc, bias)

    mu_p = _rw_layout(rw_mu_prev[lyr][None, :])
    mu_n = _rw_layout(rw_mu_next[lyr][None, :])
    zeros_up = jnp.zeros((DECAY_LORA, RW_WIDTH), F32)
    wup = jnp.concatenate([jnp.concatenate([rw_w_up[lyr, 0], zeros_up], axis=1),
                           jnp.concatenate([zeros_up, rw_w_up[lyr, 1]], axis=1)], axis=0)
    aup = jnp.concatenate([jnp.concatenate([rw_a_up[lyr, 0], rw_a_up[lyr, 1]], axis=1),
                           jnp.zeros((LANE - AAA_LORA, 2 * RW_WIDTH), F32)], axis=0)
    ri = np.arange(RW_CHUNKS_PER_STEP * CHUNK)
    same = (ri[None, :] // CHUNK) == (ri[:, None] // CHUNK)
    tri = jnp.asarray(np.stack([same & (ri[None, :] <= ri[:, None]), same & (ri[None, :] >= ri[:, None])]), BF16)
    consts = (mu_p, mu_n, rw_w0[lyr], *_hl(wup), rw_a0[lyr], *_hl(aup), *_hl(rw_g_up[lyr]),
              rw_k_k[lyr][None, :], rw_k_a[lyr][None, :], rw_r_k[lyr].reshape(1, RW_WIDTH), tri, ones_bd)

    _, _, rh_c, y0_c, m_c, n_c = _rw_pre(prw_c, consts)
    g, bon, rh, y0, m, n = _rw_pre(prw, consts)
    z0 = jnp.zeros((2, b, HEAD_DIM, RW_WIDTH), F32)
    _, _, z_ctx = _rw_scan(rh_c, y0_c, m_c, n_c, z0, ones_bd)
    yf, yr, _ = _rw_scan(rh, y0, m, n, z_ctx, ones_bd)

    wo = w_out[lyr].astype(BF16)
    d_ff = ffn_w_out.shape[1]
    w1 = ffn_w_in[lyr].astype(BF16)
    return _tail(yf, yr, bon, g, o_na, x, mod[2], mod[3], mod[4], mod[5],
                 rw_ln_g[lyr][None, :], rw_ln_b[lyr][None, :], ones_bd, wo[:NA_WIDTH], wo[NA_WIDTH:],
                 norm2_g[lyr][None, :], w1[:, :d_ff], w1[:, d_ff:], ffn_w_out[lyr].astype(BF16))
```

```python
import functools

import numpy as np
import jax
import jax.numpy as jnp
from jax import lax
from jax.experimental import pallas as pl
from jax.experimental.pallas import tpu as pltpu

F32 = jnp.float32
BF16 = jnp.bfloat16

GRID_W = 64
NA_HEADS = 8
HEAD_DIM = 64
NA_WIDTH = NA_HEADS * HEAD_DIM
NA_WIN_ROWS = 8
NA_WIN_COLS = 16
RW_HEADS = 8
RW_WIDTH = RW_HEADS * HEAD_DIM
DECAY_LORA = 64
AAA_LORA = 64
GATE_LORA = 128
NORM_EPS = 1e-6
RW_GN_EPS = 64e-5

LANE = 128
RW_COLS = 3 * RW_WIDTH + 3 * LANE
OFF_WD = 3 * RW_WIDTH
OFF_AD = OFF_WD + LANE
OFF_GD = OFF_AD + LANE
CHUNK = 64
INV_BASE = 16
RW_SCAN_CHUNKS_PER_STEP = 2
RW_CHUNKS_PER_STEP = 4
NA_ROWS_PER_STEP = 4
NA_HEADS_PER_STEP = 4
NEG_BIG = -1e30
VMEM_LIMIT = 56 * 1024 * 1024


def _dot(a, b):
    return jnp.dot(a, b, preferred_element_type=F32)


def _bdot(a, b):
    return jnp.dot(a.astype(BF16), b.astype(BF16), preferred_element_type=F32)


def _bdot_nt(a, b):
    return lax.dot_general(a.astype(BF16), b.astype(BF16), (((1,), (1,)), ((), ())),
                           preferred_element_type=F32)


def _bdot_tn(a, b):
    return lax.dot_general(a.astype(BF16), b.astype(BF16), (((0,), (0,)), ((), ())),
                           preferred_element_type=F32)


def _split2(a):
    hi = a.astype(BF16)
    lo = (a - hi.astype(F32)).astype(BF16)
    return hi, lo


def _head_sum(x, ones_grp, split=True):
    m, w = x.shape
    grp = ones_grp.shape[0]
    n = w // grp
    pieces = _split2(x) if split else (x.astype(BF16),)
    parts = [t[:, g * grp:(g + 1) * grp] for t in pieces for g in range(n)]
    res = _dot(jnp.concatenate(parts, axis=0), ones_grp)
    cols = []
    for g in range(n):
        col = res[g * m:(g + 1) * m]
        if split:
            col = col + res[(n + g) * m:(n + g + 1) * m]
        cols.append(col)
    return jnp.concatenate(cols, axis=1)


def _dot_hl3(a, w_hi, w_lo):
    m = a.shape[0]
    hi, lo = _split2(a)
    res = _dot(jnp.concatenate([hi, lo], axis=0), w_hi)
    return res[:m] + (res[m:] + _dot(hi, w_lo))


def _sigmoid(x):
    return jax.nn.sigmoid(x)


def _const_spec(shape, single=False):
    nd = len(shape)
    if single:
        return pl.BlockSpec(shape, lambda *_: (0,) * nd, pipeline_mode=pl.Buffered(1))
    return pl.BlockSpec(shape, lambda *_: (0,) * nd)


def _ada_kernel(c_ref, w_ref, b_ref, o_ref):
    c = c_ref[...]
    s = c * _sigmoid(c)
    w_hi, w_lo = _split2(w_ref[...])
    o_ref[...] = _dot_hl3(s, w_hi, w_lo) + b_ref[...]


def _ada(c_rows, w_ada, b_ada):
    rows, d = c_rows.shape
    n = w_ada.shape[1]
    tn = 1024
    return pl.pallas_call(
        _ada_kernel,
        grid=(n // tn,),
        in_specs=[pl.BlockSpec((rows, d), lambda j: (0, 0)),
                  pl.BlockSpec((d, tn), lambda j: (0, j)),
                  pl.BlockSpec((1, tn), lambda j: (0, j))],
        out_specs=pl.BlockSpec((rows, tn), lambda j: (0, j)),
        out_shape=jax.ShapeDtypeStruct((rows, n), F32),
        compiler_params=pltpu.CompilerParams(dimension_semantics=("arbitrary",), vmem_limit_bytes=VMEM_LIMIT),
        name="ada",
    )(c_rows, w_ada, b_ada)


def _inproj_kernel(x_ref, xp_ref, xn_ref, sh_ref, sc_ref, g_ref, wna_ref, wrw_ref, qg_ref, kg_ref, ones_ref,
                   mup_ref, mun_ref, q_ref, k_ref, v_ref, prw_ref):
    i = pl.program_id(1)

    def modulated(xt):
        ms = jnp.mean(xt * xt, axis=-1, keepdims=True)
        y = xt * lax.rsqrt(ms + NORM_EPS) * g_ref[...]
        return (y * (1.0 + sc_ref[0]) + sh_ref[0]).astype(BF16)

    u = modulated(x_ref[0])

    tm = u.shape[0]
    u_halo = modulated(jnp.concatenate([xp_ref[0], xn_ref[0]], axis=0))
    p_all = _dot(jnp.concatenate([u, u_halo], axis=0), wrw_ref[...])
    p = p_all[:tm]
    ph = p_all[tm:]
    halo_prev = ph[7:8] * (i > 0).astype(F32)
    halo_next = ph[8:9] * (i < pl.num_programs(1) - 1).astype(F32)
    row = lax.broadcasted_iota(jnp.int32, (tm, 1), 0)
    prev = jnp.where(row == 0, halo_prev, pltpu.roll(p, 1, 0))
    nxt = jnp.where(row == tm - 1, halo_next, pltpu.roll(p, tm - 1, 0))
    prw_ref[0] = p + mup_ref[...] * (prev - p) + mun_ref[...] * (nxt - p)

    pn = _dot(u, wna_ref[...])

    def head_norm(t, gain):
        ss = _head_sum(t * t, ones_ref[...], split=False) * (1.0 / HEAD_DIM)
        return t * lax.rsqrt(ss + NORM_EPS) * gain

    q = head_norm(pn[:, :NA_WIDTH], qg_ref[...]) * (HEAD_DIM ** -0.5)
    k = head_norm(pn[:, NA_WIDTH:2 * NA_WIDTH], kg_ref[...])
    q_ref[0] = q.astype(BF16)
    k_ref[0] = k.astype(BF16)
    v_ref[0] = pn[:, 2 * NA_WIDTH:].astype(BF16)


def _inproj(x, shift, scale, g1, w_na, w_rw, qg, kg, ones_bd, mu_p, mu_n):
    b, t, d = x.shape
    tm = min(512, t)
    sub = tm // 8
    nb8 = t // 8
    vec = pl.BlockSpec((1, 1, d), lambda bi, i: (bi, 0, 0))
    tok = lambda w: pl.BlockSpec((1, tm, w), lambda bi, i: (bi, i, 0))
    return pl.pallas_call(
        _inproj_kernel,
        grid=(b, t // tm),
        in_specs=[tok(d),
                  pl.BlockSpec((1, 8, d), lambda bi, i: (bi, jnp.maximum(i * sub - 1, 0), 0)),
                  pl.BlockSpec((1, 8, d), lambda bi, i: (bi, jnp.minimum((i + 1) * sub, nb8 - 1), 0)),
                  vec, vec, _const_spec((1, d)), _const_spec(w_na.shape), _const_spec(w_rw.shape),
                  _const_spec((1, NA_WIDTH)), _const_spec((1, NA_WIDTH)), _const_spec(ones_bd.shape),
                  _const_spec(mu_p.shape), _const_spec(mu_n.shape)],
        out_specs=[tok(NA_WIDTH), tok(NA_WIDTH), tok(NA_WIDTH), tok(RW_COLS)],
        out_shape=[jax.ShapeDtypeStruct((b, t, NA_WIDTH), BF16)] * 3
        + [jax.ShapeDtypeStruct((b, t, RW_COLS), F32)],
        compiler_params=pltpu.CompilerParams(dimension_semantics=("arbitrary", "arbitrary"),
                                             vmem_limit_bytes=VMEM_LIMIT),
        name="inproj",
    )(x, x, x, shift, scale, g1, w_na, w_rw, qg, kg, ones_bd, mu_p, mu_n)


def _na_bias_tables(rpb, rows, r_blk):
    kr_n = r_blk + NA_WIN_ROWS - 1
    nblk = rows // r_blk
    n_h, n_di, n_dj = rpb.shape
    w = GRID_W
    lo = w - NA_WIN_COLS
    strip = jnp.pad(rpb, ((0, 0), (0, 0), (lo, 2 * w - 1 - n_dj - lo)))
    skew = jnp.tile(strip, (1, 1, w + 1))[:, :, :2 * w * w].reshape(n_h, n_di, w, 2 * w)[..., :w]
    toe = skew[:, :, ::-1, :]
    j = np.arange(w)
    cs = np.clip(j - NA_WIN_COLS // 2, 0, w - NA_WIN_COLS)
    c = np.arange(w)
    cvalid = (c[None, :] >= cs[:, None]) & (c[None, :] < cs[:, None] + NA_WIN_COLS)
    toe = jnp.where(cvalid[None, None], toe, NEG_BIG)
    neg = jnp.full((n_h, w, w), NEG_BIG, F32)
    tabs = []
    for ib in (0, min(1, nblk - 1), nblk - 1):
        i0 = ib * r_blk
        base = int(np.clip(i0 - NA_WIN_ROWS // 2, 0, rows - kr_n))
        q_rows = []
        for i in range(i0, i0 + r_blk):
            rs = int(np.clip(i - NA_WIN_ROWS // 2, 0, rows - NA_WIN_ROWS))
            blocks = []
            for kr in range(base, base + kr_n):
                blocks.append(toe[:, kr - i + NA_WIN_ROWS - 1] if rs <= kr < rs + NA_WIN_ROWS else neg)
            q_rows.append(jnp.concatenate(blocks, axis=-1))
        tabs.append(jnp.concatenate(q_rows, axis=1))
    return jnp.stack(tabs)


def _na_kernel(q_ref, k_ref, v_ref, kc_ref, vc_ref, bias_ref, o_ref, *, r_blk, rows):
    i = pl.program_id(2)
    kr_n = r_blk + NA_WIN_ROWS - 1
    base = jnp.clip(i * r_blk - NA_WIN_ROWS // 2, 0, rows - kr_n)
    start = pl.multiple_of(base * GRID_W, GRID_W)
    n_pair = q_ref.shape[2] // LANE
    lane = lax.broadcasted_iota(jnp.int32, (q_ref.shape[1], LANE), 1)
    heads = [(pr, hh) for pr in range(n_pair) for hh in range(2)]
    psl = lambda pr: slice(pr * LANE, (pr + 1) * LANE)
    kw = [k_ref[0, pl.ds(start, kr_n * GRID_W), psl(pr)] for pr in range(n_pair)]
    vw = [v_ref[0, pl.ds(start, kr_n * GRID_W), psl(pr)] for pr in range(n_pair)]
    s, sc = [], []
    for pr, hh in heads:
        q = q_ref[0, :, psl(pr)]
        qh = jnp.where((lane < HEAD_DIM) if hh == 0 else (lane >= HEAD_DIM), q, jnp.zeros_like(q))
        s.append(_bdot_nt(qh, kw[pr]) + bias_ref[0, 2 * pr + hh])
        sc.append(_bdot_nt(qh, kc_ref[0, :, psl(pr)]))
    outs = []
    for n, (pr, hh) in enumerate(heads):
        m = jnp.maximum(jnp.max(s[n], axis=-1, keepdims=True), jnp.max(sc[n], axis=-1, keepdims=True))
        p = jnp.exp(s[n] - m)
        pc = jnp.exp(sc[n] - m)
        l = jnp.sum(p, axis=-1, keepdims=True) + jnp.sum(pc, axis=-1, keepdims=True)
        o = _bdot(p, vw[pr]) + _bdot(pc, vc_ref[0, :, psl(pr)])
        outs.append(o / l)
    for pr in range(n_pair):
        o_ref[0, :, psl(pr)] = jnp.where(lane < HEAD_DIM, outs[2 * pr], outs[2 * pr + 1]).astype(o_ref.dtype)


def _na(q, k, v, kc, vc, bias):
    b, t, _ = q.shape
    l_ctx = kc.shape[1]
    rows = t // GRID_W
    r_blk = NA_ROWS_PER_STEP
    kr_n = r_blk + NA_WIN_ROWS - 1
    nblk = rows // r_blk
    assert rows % r_blk == 0 and rows >= kr_n and t % GRID_W == 0
    w = NA_HEADS_PER_STEP * HEAD_DIM
    n_grp = NA_WIDTH // w

    def bias_map(bi, hg, i):
        case = jnp.where(i == 0, 0, jnp.where(i == nblk - 1, 2, 1))
        return (case, hg, 0, 0)

    whole = lambda rows_: pl.BlockSpec((1, rows_, w), lambda bi, hg, i: (bi, 0, hg), pipeline_mode=pl.Buffered(1))
    return pl.pallas_call(
        functools.partial(_na_kernel, r_blk=r_blk, rows=rows),
        grid=(b, n_grp, nblk),
        in_specs=[pl.BlockSpec((1, r_blk * GRID_W, w), lambda bi, hg, i: (bi, i, hg)),
                  whole(t), whole(t), whole(l_ctx), whole(l_ctx),
                  pl.BlockSpec((1, NA_HEADS_PER_STEP, r_blk * GRID_W, kr_n * GRID_W), bias_map)],
        out_specs=pl.BlockSpec((1, r_blk * GRID_W, w), lambda bi, hg, i: (bi, i, hg)),
        out_shape=jax.ShapeDtypeStruct((b, t, NA_WIDTH), BF16),
        compiler_params=pltpu.CompilerParams(dimension_semantics=("arbitrary",) * 3, vmem_limit_bytes=VMEM_LIMIT),
        name="na",
    )(q, k, v, kc, vc, bias)


def _rw_pre_kernel(p_ref, w0_ref, wup_hi_ref, wup_lo_ref, a0_ref, aup_ref,
                   gup_ref, kkw_ref, ka_ref, rk_ref, tri_ref, bd_ref,
                   g_ref, bon_ref, rh_ref, y0_ref, m_ref, n_ref):
    cl = CHUNK
    ps = p_ref[0]
    tm = ps.shape[0]

    r = ps[:, 0:RW_WIDTH]
    k = ps[:, RW_WIDTH:2 * RW_WIDTH]
    v = ps[:, 2 * RW_WIDTH:3 * RW_WIDTH]
    wd = ps[:, OFF_WD:OFF_WD + LANE]
    ad = ps[:, OFF_AD:OFF_AD + LANE]
    gd = ps[:, OFF_GD:OFF_GD + LANE]

    lora_w = _dot_hl3(jnp.tanh(wd), wup_hi_ref[...], wup_lo_ref[...])
    lora_a = _bdot(ad, aup_ref[...])
    g_ref[0] = _bdot(_sigmoid(gd), gup_ref[...])

    bd_mask = bd_ref[...]
    kkv = k * kkw_ref[...]
    kk = kkv * lax.rsqrt(jnp.maximum(_head_sum(kkv * kkv, bd_mask), 1e-24))

    grp = 4 * HEAD_DIM
    n_grp = RW_WIDTH // grp
    ri = lax.broadcasted_iota(jnp.int32, (cl, grp), 0)
    ci = lax.broadcasted_iota(jnp.int32, (cl, grp), 1) & (HEAD_DIM - 1)
    eye_f = (ri == ci).astype(F32)
    n_sq = int(np.log2(INV_BASE)) - 1
    blk = []
    size = INV_BASE
    while size <= cl:
        sh = int(np.log2(size))
        blk.append((ri >> sh) == (ci >> sh))
        size *= 2
    strict = ((ci < ri), (ci > ri))
    incl = ((ci <= ri), (ci >= ri))

    def bd(x):
        xb = x.astype(BF16)
        return jnp.concatenate([xb] * (grp // cl), axis=0) * bd_mask

    n_ch = tm // cl
    csl = lambda ch: slice(ch * cl, (ch + 1) * cl)
    bonus_terms = jnp.zeros((tm, RW_WIDTH), F32)
    per_dir = []
    for d in range(2):
        sl_d = slice(d * RW_WIDTH, (d + 1) * RW_WIDTH)
        z = w0_ref[d:d + 1, :] + lora_w[:, sl_d]
        lw = _sigmoid(z) * (-float(np.exp(-0.5)))
        a = _sigmoid(a0_ref[d:d + 1, :] + lora_a[:, sl_d])
        kd = k * (1.0 + (a - 1.0) * ka_ref[...])
        bvec = kk * a
        bonus_terms = bonus_terms + r * kd * rk_ref[...]

        l_hi, l_lo = _split2(lw)
        tri = tri_ref[d]
        cum = _dot(tri, l_hi) + _dot(tri, l_lo)
        tot = [jnp.sum(lw[csl(ch)], axis=0, keepdims=True) for ch in range(n_ch)]
        tot_rows = jnp.concatenate([jnp.broadcast_to(tc, (cl, RW_WIDTH)) for tc in tot], axis=0)
        e_neg = jnp.exp(-cum)
        e_tot = jnp.exp(tot_rows - cum)
        per_dir.append(dict(at=(-kk * jnp.exp(cum - lw)).astype(BF16), rt=r * jnp.exp(cum),
                            bt=(bvec * e_neg).astype(BF16), kt=(kd * e_neg).astype(BF16),
                            bb=(bvec * e_tot).astype(BF16), kb=(kd * e_tot).astype(BF16),
                            p_tot=[jnp.exp(tc) for tc in tot]))
    bon_ref[0] = _head_sum(bonus_terms, bd_mask, split=False) * v
    v_bf = v.astype(BF16)

    probs = [(ch, d, g) for ch in range(n_ch) for d in range(2) for g in range(n_grp)]
    rng = range(len(probs))
    gsl = lambda g: slice(g * grp, (g + 1) * grp)
    at = [per_dir[d]["at"][csl(ch), gsl(g)] for ch, d, g in probs]
    rt = [per_dir[d]["rt"][csl(ch), gsl(g)] for ch, d, g in probs]
    bd_v = {(ch, g): bd(v_bf[csl(ch), gsl(g)]) for ch in range(n_ch) for g in range(n_grp)}
    lhs = [jnp.concatenate([at[i], rt[i].astype(BF16)], axis=0) for i in rng]
    s_b = [_bdot_nt(lhs[i], bd(per_dir[d]["bt"][csl(ch), gsl(g)])) for i, (ch, d, g) in enumerate(probs)]
    s_k = [_bdot_nt(lhs[i], bd(per_dir[d]["kt"][csl(ch), gsl(g)])) for i, (ch, d, g) in enumerate(probs)]
    aab = [jnp.where(strict[probs[i][1]], s_b[i][:cl], 0.0) for i in rng]
    arb = [jnp.where(incl[probs[i][1]], s_b[i][cl:], 0.0).astype(BF16) for i in rng]
    aak = [jnp.where(strict[probs[i][1]], s_k[i][:cl], 0.0).astype(BF16) for i in rng]
    ark = [jnp.where(incl[probs[i][1]], s_k[i][cl:], 0.0).astype(BF16) for i in rng]
    d0 = [jnp.where(blk[0], aab[i], 0.0) for i in rng]
    tinv = [d0[i] + eye_f for i in rng]
    apow = [_bdot(d0[i], bd(d0[i])) for i in rng]
    for _ in range(n_sq - 1):
        res = [_bdot(jnp.concatenate([tinv[i], apow[i]], axis=0), bd(apow[i])) for i in rng]
        tinv = [tinv[i] + res[i][:cl] for i in rng]
        apow = [res[i][cl:] for i in rng]
    tinv = [tinv[i] + _bdot(tinv[i], bd(apow[i])) for i in rng]
    for lvl in range(1, len(blk)):
        sel = jnp.logical_and(blk[lvl], jnp.logical_not(blk[lvl - 1]))
        tmp = [_bdot(jnp.where(sel, aab[i], 0.0), bd(tinv[i])) for i in rng]
        tinv = [tinv[i] + _bdot(tinv[i], bd(tmp[i])) for i in rng]
    tinv = [tinv[i].astype(BF16) for i in rng]
    ah = [_bdot(tinv[i], bd(at[i])).astype(BF16) for i in rng]
    av = [_bdot(aak[i], bd_v[probs[i][0], probs[i][2]]) for i in rng]
    wh = [_bdot(tinv[i], bd(av[i])).astype(BF16) for i in rng]
    for i, (ch, d, g) in enumerate(probs):
        rh_ref[d, 0, csl(ch), gsl(g)] = (rt[i] + _bdot(arb[i], bd(ah[i]))).astype(rh_ref.dtype)
    for i, (ch, d, g) in enumerate(probs):
        y0_ref[d, 0, csl(ch), gsl(g)] = _bdot(arb[i], bd(wh[i])) + _bdot(ark[i], bd_v[ch, g])
    r2 = lax.broadcasted_iota(jnp.int32, (grp, grp), 0) >> int(np.log2(HEAD_DIM))
    c2 = lax.broadcasted_iota(jnp.int32, (grp, grp), 1) >> int(np.log2(HEAD_DIM))
    same_head = r2 == c2

    def fold(full):
        fm = jnp.where(same_head, full, 0.0)
        out = fm[0:HEAD_DIM]
        for j in range(1, grp // HEAD_DIM):
            out = out + fm[j * HEAD_DIM:(j + 1) * HEAD_DIM]
        return out

    for i, (ch, d, g) in enumerate(probs):
        bb = per_dir[d]["bb"][csl(ch), gsl(g)]
        diag = eye_f * per_dir[d]["p_tot"][ch][:, gsl(g)]
        m_ref[d, 0, ch, :, gsl(g)] = (diag + fold(_bdot_tn(bb, ah[i]))).astype(m_ref.dtype)
    for i, (ch, d, g) in enumerate(probs):
        bk = jnp.concatenate([per_dir[d]["bb"][csl(ch), gsl(g)], per_dir[d]["kb"][csl(ch), gsl(g)]], axis=0)
        wv = jnp.concatenate([wh[i], v_bf[csl(ch), gsl(g)]], axis=0)
        n_ref[d, 0, ch, :, gsl(g)] = fold(_bdot_tn(bk, wv))


def _rw_pre(prw, consts):
    b, t, _ = prw.shape
    cl = CHUNK
    cps = RW_CHUNKS_PER_STEP
    tm = cps * cl
    nc = t // cl
    assert t % tm == 0 and tm % 8 == 0
    tok = lambda w: pl.BlockSpec((1, tm, w), lambda bi, c: (bi, c, 0))
    in_specs = [tok(RW_COLS)] + [_const_spec(a.shape) for a in consts]
    dirtok = pl.BlockSpec((2, 1, tm, RW_WIDTH), lambda bi, c: (0, bi, c, 0))
    mat = pl.BlockSpec((2, 1, cps, HEAD_DIM, RW_WIDTH), lambda bi, c: (0, bi, c, 0, 0))
    return pl.pallas_call(
        _rw_pre_kernel,
        grid=(b, t // tm),
        in_specs=in_specs,
        out_specs=[tok(RW_WIDTH), tok(RW_WIDTH), dirtok, dirtok, mat, mat],
        out_shape=[jax.ShapeDtypeStruct((b, t, RW_WIDTH), F32)] * 2
        + [jax.ShapeDtypeStruct((2, b, t, RW_WIDTH), dt) for dt in (BF16, F32)]
        + [jax.ShapeDtypeStruct((2, b, nc, HEAD_DIM, RW_WIDTH), dt) for dt in (BF16, F32)],
        compiler_params=pltpu.CompilerParams(dimension_semantics=("arbitrary", "arbitrary"),
                                             vmem_limit_bytes=VMEM_LIMIT),
        name="rw_pre",
    )(prw, *consts)


def _rw_scan_kernel(rhf_ref, y0f_ref, mf_ref, nf_ref, rhr_ref, y0r_ref, mr_ref, nr_ref, z0_ref, bd_ref,
                    yf_ref, yr_ref, zf_ref, z_scr):
    j = pl.program_id(0)

    @pl.when(j == 0)
    def _():
        z_scr[...] = z0_ref[...]

    cl = CHUNK
    nb = z_scr.shape[1]
    n_ch = rhf_ref.shape[2] // cl
    bd_mask = bd_ref[...]
    grp = bd_mask.shape[0]
    dirs = ((rhf_ref, y0f_ref, mf_ref, nf_ref, yf_ref), (rhr_ref, y0r_ref, mr_ref, nr_ref, yr_ref))
    for step in range(n_ch):
        for d, (rh_ref, y0_ref, m_ref, n_ref, y_ref) in enumerate(dirs):
            ch = step if d == 0 else n_ch - 1 - step
            rows = slice(ch * cl, (ch + 1) * cl)
            for bi in range(nb):
                for g in range(RW_WIDTH // grp):
                    gs = slice(g * grp, (g + 1) * grp)
                    z_bd = jnp.concatenate([z_scr[d, bi, :, gs].astype(BF16)] * (grp // HEAD_DIM), axis=0) * bd_mask
                    lhs = jnp.concatenate([rh_ref[0, bi, rows, gs], m_ref[0, bi, ch, :, gs]], axis=0)
                    res = _dot(lhs, z_bd)
                    y_ref[bi, rows, gs] = res[:cl] + y0_ref[0, bi, rows, gs]
                    z_scr[d, bi, :, gs] = res[cl:] + n_ref[0, bi, ch, :, gs]

    @pl.when(j == pl.num_programs(0) - 1)
    def _():
        zf_ref[...] = z_scr[...]


def _rw_scan(rh, y0, m, n, z0, bd_mask):
    _, b, t, _ = rh.shape
    cl = CHUNK
    cps = RW_SCAN_CHUNKS_PER_STEP
    tm = cps * cl
    ns = t // tm
    assert t % tm == 0
    tokf = pl.BlockSpec((1, b, tm, RW_WIDTH), lambda j: (0, 0, j, 0))
    tokr = pl.BlockSpec((1, b, tm, RW_WIDTH), lambda j: (1, 0, ns - 1 - j, 0))
    matf = pl.BlockSpec((1, b, cps, HEAD_DIM, RW_WIDTH), lambda j: (0, 0, j, 0, 0))
    matr = pl.BlockSpec((1, b, cps, HEAD_DIM, RW_WIDTH), lambda j: (1, 0, ns - 1 - j, 0, 0))
    zspec = _const_spec(z0.shape)
    return pl.pallas_call(
        _rw_scan_kernel,
        grid=(ns,),
        in_specs=[tokf, tokf, matf, matf, tokr, tokr, matr, matr, zspec, _const_spec(bd_mask.shape)],
        out_specs=[pl.BlockSpec((b, tm, RW_WIDTH), lambda j: (0, j, 0)),
                   pl.BlockSpec((b, tm, RW_WIDTH), lambda j: (0, ns - 1 - j, 0)),
                   zspec],
        out_shape=[jax.ShapeDtypeStruct((b, t, RW_WIDTH), F32)] * 2 + [jax.ShapeDtypeStruct(z0.shape, F32)],
        scratch_shapes=[pltpu.VMEM(z0.shape, F32)],
        compiler_params=pltpu.CompilerParams(dimension_semantics=("arbitrary",), vmem_limit_bytes=VMEM_LIMIT),
        name="rw_scan",
    )(rh, y0, m, n, rh, y0, m, n, z0, bd_mask)


def _tail_kernel(yf_ref, yr_ref, bon_ref, g_ref, ona_ref, x_ref, gate1_ref, sh_ref, sc_ref, gate2_ref,
                 lng_ref, lnb_ref, ones_ref, wona_ref, worw_ref, g2_ref, w1g_ref, w1u_ref, w2_ref, o_ref, *, ff_chunk):
    ones = ones_ref[...]
    wkv = yf_ref[0] + yr_ref[0]
    mu = _head_sum(wkv, ones, split=False) * (1.0 / HEAD_DIM)
    dlt = wkv - mu
    var = _head_sum(dlt * dlt, ones, split=False) * (1.0 / HEAD_DIM)
    yn = dlt * lax.rsqrt(var + RW_GN_EPS) * lng_ref[...] + lnb_ref[...]
    orw = ((yn + bon_ref[0]) * g_ref[0]).astype(BF16)
    mix = _dot(ona_ref[0], wona_ref[...]) + _dot(orw, worw_ref[...])
    h1 = x_ref[0] + gate1_ref[0] * mix
    ms = jnp.mean(h1 * h1, axis=-1, keepdims=True)
    u = (h1 * lax.rsqrt(ms + NORM_EPS) * g2_ref[...] * (1.0 + sc_ref[0]) + sh_ref[0]).astype(BF16)
    bounds = _ff_bounds(w2_ref.shape[0], ff_chunk)

    def up_proj(lo, hi):
        return _dot(u, w1g_ref[:, lo:hi]), _dot(u, w1u_ref[:, lo:hi])

    acc = jnp.zeros(h1.shape, F32)
    nxt = up_proj(*bounds[0])
    for ci, (lo, hi) in enumerate(bounds):
        gt, up = nxt
        if ci + 1 < len(bounds):
            nxt = up_proj(*bounds[ci + 1])
        act = (gt * _sigmoid(gt) * up).astype(BF16)
        acc = acc + _dot(act, w2_ref[lo:hi, :])
    o_ref[0] = h1 + gate2_ref[0] * acc


FF_CHUNK = 512


def _ff_bounds(d_ff, ff_chunk):
    edges = list(range(0, d_ff, ff_chunk)) + [d_ff]
    return [(lo, hi) for lo, hi in zip(edges[:-1], edges[1:])]


def _tail(yf, yr, bon, g, ona, x, gate1, sh, sc, gate2, lng, lnb, ones_bd, wona, worw, g2, w1g, w1u, w2):
    b, t, d = x.shape
    tm = min(512, t)
    vec = pl.BlockSpec((1, 1, d), lambda bi, i: (bi, 0, 0))
    tok = lambda w: pl.BlockSpec((1, tm, w), lambda bi, i: (bi, i, 0))
    consts = (lng, lnb, ones_bd, wona, worw, g2, w1g, w1u, w2)
    return pl.pallas_call(
        functools.partial(_tail_kernel, ff_chunk=FF_CHUNK),
        grid=(b, t // tm),
        in_specs=[tok(RW_WIDTH)] * 4 + [tok(NA_WIDTH), tok(d), vec, vec, vec, vec]
        + [_const_spec(a.shape, single=True) for a in consts],
        out_specs=tok(d),
        out_shape=jax.ShapeDtypeStruct((b, t, d), F32),
        compiler_params=pltpu.CompilerParams(dimension_semantics=("arbitrary", "arbitrary"),
                                             vmem_limit_bytes=VMEM_LIMIT),
        name="tail",
    )(yf, yr, bon, g, ona, x, gate1, sh, sc, gate2, *consts)


def _pad_cols(w, width):
    return jnp.pad(w, ((0, 0), (0, width - w.shape[1])))


def _rw_layout(w):
    o = 3 * RW_WIDTH
    wd = w[:, o:o + 2 * DECAY_LORA]
    ad = _pad_cols(w[:, o + 2 * DECAY_LORA:o + 2 * DECAY_LORA + AAA_LORA], LANE)
    gd = w[:, o + 2 * DECAY_LORA + AAA_LORA:]
    return jnp.concatenate([w[:, :o], wd, ad, gd], axis=1)


def _hl(w):
    hi = w.astype(BF16)
    return hi, (w - hi.astype(F32)).astype(BF16)


def kernel(x, c, ctx, c_ctx, norm1_g, norm2_g, w_ada, b_ada, w_in, na_q_g, na_k_g, na_rpb, rw_mu_prev, rw_mu_next,
           rw_w0, rw_w_up, rw_a0, rw_a_up, rw_g_up, rw_k_k, rw_k_a, rw_r_k, rw_ln_g, rw_ln_b, w_out, ffn_w_in,
           ffn_w_out):
    depth = w_in.shape[0]
    assert depth == 1, "single-layer kernel"
    b, t, d = x.shape
    l_ctx = ctx.shape[1]
    lyr = 0

    n_rows = -(-(b + 1) // 8) * 8
    c_rows = jnp.zeros((n_rows, d), F32).at[:b].set(c).at[b].set(c_ctx)
    mod_all = _ada(c_rows, w_ada[lyr], b_ada[lyr][None, :])
    mod = [mod_all[:b, i * d:(i + 1) * d][:, None, :] for i in range(6)]
    modc = [jnp.broadcast_to(mod_all[b, i * d:(i + 1) * d][None, None, :], (b, 1, d)) for i in range(2)]

    w_na = w_in[lyr][:, :3 * NA_WIDTH].astype(BF16)
    w_rw = _rw_layout(w_in[lyr][:, 3 * NA_WIDTH:]).astype(BF16)
    hd = np.arange(4 * HEAD_DIM) // HEAD_DIM
    ones_bd = jnp.asarray(hd[:, None] == hd[None, :], BF16)
    qg = jnp.tile(na_q_g[lyr], NA_HEADS)[None, :]
    kg = jnp.tile(na_k_g[lyr], NA_HEADS)[None, :]
    g1 = norm1_g[lyr][None, :]

    mu_p = _rw_layout(rw_mu_prev[lyr][None, :])
    mu_n = _rw_layout(rw_mu_next[lyr][None, :])
    q, k, v, prw = _inproj(x, mod[0], mod[1], g1, w_na, w_rw, qg, kg, ones_bd, mu_p, mu_n)
    _, kc, vc, prw_c = _inproj(ctx, modc[0], modc[1], g1, w_na, w_rw, qg, kg, ones_bd, mu_p, mu_n)

    bias = _na_bias_tables(na_rpb[lyr], t // GRID_W, NA_ROWS_PER_STEP)
    o_na = _na(q, k, v, kc, vc, bias)

    zeros_up = jnp.zeros((DECAY_LORA, RW_WIDTH), F32)
    wup = jnp.concatenate([jnp.concatenate([rw_w_up[lyr, 0], zeros_up], axis=1),
                           jnp.concatenate([zeros_up, rw_w_up[lyr, 1]], axis=1)], axis=0)
    aup = jnp.concatenate([jnp.concatenate([rw_a_up[lyr, 0], rw_a_up[lyr, 1]], axis=1),
                           jnp.zeros((LANE - AAA_LORA, 2 * RW_WIDTH), F32)], axis=0)
    ri = np.arange(RW_CHUNKS_PER_STEP * CHUNK)
    same = (ri[None, :] // CHUNK) == (ri[:, None] // CHUNK)
    tri = jnp.asarray(np.stack([same & (ri[None, :] <= ri[:, None]), same & (ri[None, :] >= ri[:, None])]), BF16)
    consts = (rw_w0[lyr], *_hl(wup), rw_a0[lyr], aup.astype(BF16), rw_g_up[lyr].astype(BF16),
              rw_k_k[lyr][None, :], rw_k_a[lyr][None, :], rw_r_k[lyr].reshape(1, RW_WIDTH), tri, ones_bd)

    _, _, rh_c, y0_c, m_c, n_c = _rw_pre(prw_c, consts)
    g, bon, rh, y0, m, n = _rw_pre(prw, consts)
    z0 = jnp.zeros((2, b, HEAD_DIM, RW_WIDTH), F32)
    _, _, z_ctx = _rw_scan(rh_c, y0_c, m_c, n_c, z0, ones_bd)
    yf, yr, _ = _rw_scan(rh, y0, m, n, z_ctx, ones_bd)

    wo = w_out[lyr].astype(BF16)
    d_ff = ffn_w_out.shape[1]
    w1 = ffn_w_in[lyr].astype(BF16)
    return _tail(yf, yr, bon, g, o_na, x, mod[2], mod[3], mod[4], mod[5],
                 rw_ln_g[lyr][None, :], rw_ln_b[lyr][None, :], ones_bd, wo[:NA_WIDTH], wo[NA_WIDTH:],
                 norm2_g[lyr][None, :], w1[:, :d_ff], w1[:, d_ff:], ffn_w_out[lyr].astype(BF16))
```

```python
import functools

import numpy as np
import jax
import jax.numpy as jnp
from jax import lax
from jax.experimental import pallas as pl
from jax.experimental.pallas import tpu as pltpu

F32 = jnp.float32
BF16 = jnp.bfloat16

GRID_W = 64
NA_HEADS = 8
HEAD_DIM = 64
NA_WIDTH = NA_HEADS * HEAD_DIM
NA_WIN_ROWS = 8
NA_WIN_COLS = 16
RW_HEADS = 8
RW_WIDTH = RW_HEADS * HEAD_DIM
DECAY_LORA = 64
AAA_LORA = 64
GATE_LORA = 128
NORM_EPS = 1e-6
RW_GN_EPS = 64e-5

LANE = 128
RW_COLS = 3 * RW_WIDTH + 3 * LANE
OFF_WD = 3 * RW_WIDTH
OFF_AD = OFF_WD + LANE
OFF_GD = OFF_AD + LANE
CHUNK = 64
INV_BASE = 16
RW_SCAN_CHUNKS_PER_STEP = 2
RW_CHUNKS_PER_STEP = 4
NA_ROWS_PER_STEP = 4
NA_HEADS_PER_STEP = 4
NEG_BIG = -1e30
VMEM_LIMIT = 56 * 1024 * 1024


def _dot(a, b):
    return jnp.dot(a, b, preferred_element_type=F32)


def _bdot(a, b):
    return jnp.dot(a.astype(BF16), b.astype(BF16), preferred_element_type=F32)


def _bdot_nt(a, b):
    return lax.dot_general(a.astype(BF16), b.astype(BF16), (((1,), (1,)), ((), ())),
                           preferred_element_type=F32)


def _bdot_tn(a, b):
    return lax.dot_general(a.astype(BF16), b.astype(BF16), (((0,), (0,)), ((), ())),
                           preferred_element_type=F32)


def _split2(a):
    hi = a.astype(BF16)
    lo = (a - hi.astype(F32)).astype(BF16)
    return hi, lo


def _head_sum(x, ones_grp, split=True):
    m, w = x.shape
    grp = ones_grp.shape[0]
    n = w // grp
    pieces = _split2(x) if split else (x.astype(BF16),)
    parts = [t[:, g * grp:(g + 1) * grp] for t in pieces for g in range(n)]
    res = _dot(jnp.concatenate(parts, axis=0), ones_grp)
    cols = []
    for g in range(n):
        col = res[g * m:(g + 1) * m]
        if split:
            col = col + res[(n + g) * m:(n + g + 1) * m]
        cols.append(col)
    return jnp.concatenate(cols, axis=1)


def _dot_hl3(a, w_hi, w_lo):
    m = a.shape[0]
    hi, lo = _split2(a)
    res = _dot(jnp.concatenate([hi, lo], axis=0), w_hi)
    return res[:m] + (res[m:] + _dot(hi, w_lo))


def _sigmoid(x):
    return jax.nn.sigmoid(x)


def _const_spec(shape, single=False):
    nd = len(shape)
    if single:
        return pl.BlockSpec(shape, lambda *_: (0,) * nd, pipeline_mode=pl.Buffered(1))
    return pl.BlockSpec(shape, lambda *_: (0,) * nd)


def _ada_kernel(c_ref, w_ref, b_ref, o_ref):
    c = c_ref[...]
    s = c * _sigmoid(c)
    w_hi, w_lo = _split2(w_ref[...])
    o_ref[...] = _dot_hl3(s, w_hi, w_lo) + b_ref[...]


def _ada(c_rows, w_ada, b_ada):
    rows, d = c_rows.shape
    n = w_ada.shape[1]
    tn = 1024
    return pl.pallas_call(
        _ada_kernel,
        grid=(n // tn,),
        in_specs=[pl.BlockSpec((rows, d), lambda j: (0, 0)),
                  pl.BlockSpec((d, tn), lambda j: (0, j)),
                  pl.BlockSpec((1, tn), lambda j: (0, j))],
        out_specs=pl.BlockSpec((rows, tn), lambda j: (0, j)),
        out_shape=jax.ShapeDtypeStruct((rows, n), F32),
        compiler_params=pltpu.CompilerParams(dimension_semantics=("arbitrary",), vmem_limit_bytes=VMEM_LIMIT),
        name="ada",
    )(c_rows, w_ada, b_ada)


def _inproj_kernel(x_ref, xp_ref, xn_ref, sh_ref, sc_ref, g_ref, wna_ref, wrw_ref, qg_ref, kg_ref, ones_ref,
                   mup_ref, mun_ref, q_ref, k_ref, v_ref, prw_ref):
    i = pl.program_id(1)

    def modulated(xt):
        ms = jnp.mean(xt * xt, axis=-1, keepdims=True)
        y = xt * lax.rsqrt(ms + NORM_EPS) * g_ref[...]
        return (y * (1.0 + sc_ref[0]) + sh_ref[0]).astype(BF16)

    u = modulated(x_ref[0])

    tm = u.shape[0]
    u_halo = modulated(jnp.concatenate([xp_ref[0], xn_ref[0]], axis=0))
    p_all = _dot(jnp.concatenate([u, u_halo], axis=0), wrw_ref[...])
    p = p_all[:tm]
    ph = p_all[tm:]
    halo_prev = ph[7:8] * (i > 0).astype(F32)
    halo_next = ph[8:9] * (i < pl.num_programs(1) - 1).astype(F32)
    row = lax.broadcasted_iota(jnp.int32, (tm, 1), 0)
    prev = jnp.where(row == 0, halo_prev, pltpu.roll(p, 1, 0))
    nxt = jnp.where(row == tm - 1, halo_next, pltpu.roll(p, tm - 1, 0))
    prw_ref[0] = p + mup_ref[...] * (prev - p) + mun_ref[...] * (nxt - p)

    pn = _dot(u, wna_ref[...])

    def head_norm(t, gain):
        ss = _head_sum(t * t, ones_ref[...], split=False) * (1.0 / HEAD_DIM)
        return t * lax.rsqrt(ss + NORM_EPS) * gain

    q = head_norm(pn[:, :NA_WIDTH], qg_ref[...]) * (HEAD_DIM ** -0.5)
    k = head_norm(pn[:, NA_WIDTH:2 * NA_WIDTH], kg_ref[...])
    q_ref[0] = q.astype(BF16)
    k_ref[0] = k.astype(BF16)
    v_ref[0] = pn[:, 2 * NA_WIDTH:].astype(BF16)


def _inproj(x, shift, scale, g1, w_na, w_rw, qg, kg, ones_bd, mu_p, mu_n):
    b, t, d = x.shape
    tm = min(512, t)
    sub = tm // 8
    nb8 = t // 8
    vec = pl.BlockSpec((1, 1, d), lambda bi, i: (bi, 0, 0))
    tok = lambda w: pl.BlockSpec((1, tm, w), lambda bi, i: (bi, i, 0))
    return pl.pallas_call(
        _inproj_kernel,
        grid=(b, t // tm),
        in_specs=[tok(d),
                  pl.BlockSpec((1, 8, d), lambda bi, i: (bi, jnp.maximum(i * sub - 1, 0), 0)),
                  pl.BlockSpec((1, 8, d), lambda bi, i: (bi, jnp.minimum((i + 1) * sub, nb8 - 1), 0)),
                  vec, vec, _const_spec((1, d)), _const_spec(w_na.shape), _const_spec(w_rw.shape),
                  _const_spec((1, NA_WIDTH)), _const_spec((1, NA_WIDTH)), _const_spec(ones_bd.shape),
                  _const_spec(mu_p.shape), _const_spec(mu_n.shape)],
        out_specs=[tok(NA_WIDTH), tok(NA_WIDTH), tok(NA_WIDTH), tok(RW_COLS)],
        out_shape=[jax.ShapeDtypeStruct((b, t, NA_WIDTH), BF16)] * 3
        + [jax.ShapeDtypeStruct((b, t, RW_COLS), F32)],
        compiler_params=pltpu.CompilerParams(dimension_semantics=("arbitrary", "arbitrary"),
                                             vmem_limit_bytes=VMEM_LIMIT),
        name="inproj",
    )(x, x, x, shift, scale, g1, w_na, w_rw, qg, kg, ones_bd, mu_p, mu_n)


def _na_bias_tables(rpb, rows, r_blk):
    kr_n = r_blk + NA_WIN_ROWS - 1
    nblk = rows // r_blk
    n_h, n_di, n_dj = rpb.shape
    w = GRID_W
    lo = w - NA_WIN_COLS
    strip = jnp.pad(rpb, ((0, 0), (0, 0), (lo, 2 * w - 1 - n_dj - lo)))
    skew = jnp.tile(strip, (1, 1, w + 1))[:, :, :2 * w * w].reshape(n_h, n_di, w, 2 * w)[..., :w]
    toe = skew[:, :, ::-1, :]
    j = np.arange(w)
    cs = np.clip(j - NA_WIN_COLS // 2, 0, w - NA_WIN_COLS)
    c = np.arange(w)
    cvalid = (c[None, :] >= cs[:, None]) & (c[None, :] < cs[:, None] + NA_WIN_COLS)
    toe = jnp.where(cvalid[None, None], toe, NEG_BIG)
    neg = jnp.full((n_h, w, w), NEG_BIG, F32)
    tabs = []
    for ib in (0, min(1, nblk - 1), nblk - 1):
        i0 = ib * r_blk
        base = int(np.clip(i0 - NA_WIN_ROWS // 2, 0, rows - kr_n))
        q_rows = []
        for i in range(i0, i0 + r_blk):
            rs = int(np.clip(i - NA_WIN_ROWS // 2, 0, rows - NA_WIN_ROWS))
            blocks = []
            for kr in range(base, base + kr_n):
                blocks.append(toe[:, kr - i + NA_WIN_ROWS - 1] if rs <= kr < rs + NA_WIN_ROWS else neg)
            q_rows.append(jnp.concatenate(blocks, axis=-1))
        tabs.append(jnp.concatenate(q_rows, axis=1))
    return jnp.stack(tabs)


def _na_kernel(q_ref, k_ref, v_ref, kc_ref, vc_ref, bias_ref, o_ref, *, r_blk, rows):
    i = pl.program_id(2)
    kr_n = r_blk + NA_WIN_ROWS - 1
    base = jnp.clip(i * r_blk - NA_WIN_ROWS // 2, 0, rows - kr_n)
    start = pl.multiple_of(base * GRID_W, GRID_W)
    n_pair = q_ref.shape[2] // LANE
    nq = q_ref.shape[1]
    lane = lax.broadcasted_iota(jnp.int32, (nq, LANE), 1)
    psl = lambda pr: slice(pr * LANE, (pr + 1) * LANE)
    s, sc = [], []
    for pr in range(n_pair):
        q = q_ref[0, :, psl(pr)]
        q2 = jnp.concatenate([jnp.where(lane < HEAD_DIM, q, jnp.zeros_like(q)),
                              jnp.where(lane >= HEAD_DIM, q, jnp.zeros_like(q))], axis=0)
        bias2 = jnp.concatenate([bias_ref[0, 2 * pr], bias_ref[0, 2 * pr + 1]], axis=0)
        s.append(_bdot_nt(q2, k_ref[0, pl.ds(start, kr_n * GRID_W), psl(pr)]) + bias2)
        sc.append(_bdot_nt(q2, kc_ref[0, :, psl(pr)]))
    for pr in range(n_pair):
        m = jnp.maximum(jnp.max(s[pr], axis=-1, keepdims=True), jnp.max(sc[pr], axis=-1, keepdims=True))
        p = jnp.exp(s[pr] - m)
        pc = jnp.exp(sc[pr] - m)
        l = jnp.sum(p, axis=-1, keepdims=True) + jnp.sum(pc, axis=-1, keepdims=True)
        o = _bdot(p, v_ref[0, pl.ds(start, kr_n * GRID_W), psl(pr)]) + _bdot(pc, vc_ref[0, :, psl(pr)])
        o = o / l
        o_ref[0, :, psl(pr)] = jnp.where(lane < HEAD_DIM, o[:nq], o[nq:]).astype(o_ref.dtype)


def _na(q, k, v, kc, vc, bias):
    b, t, _ = q.shape
    l_ctx = kc.shape[1]
    rows = t // GRID_W
    r_blk = NA_ROWS_PER_STEP
    kr_n = r_blk + NA_WIN_ROWS - 1
    nblk = rows // r_blk
    assert rows % r_blk == 0 and rows >= kr_n and t % GRID_W == 0
    w = NA_HEADS_PER_STEP * HEAD_DIM
    n_grp = NA_WIDTH // w

    def bias_map(bi, hg, i):
        case = jnp.where(i == 0, 0, jnp.where(i == nblk - 1, 2, 1))
        return (case, hg, 0, 0)

    whole = lambda rows_: pl.BlockSpec((1, rows_, w), lambda bi, hg, i: (bi, 0, hg), pipeline_mode=pl.Buffered(1))
    return pl.pallas_call(
        functools.partial(_na_kernel, r_blk=r_blk, rows=rows),
        grid=(b, n_grp, nblk),
        in_specs=[pl.BlockSpec((1, r_blk * GRID_W, w), lambda bi, hg, i: (bi, i, hg)),
                  whole(t), whole(t), whole(l_ctx), whole(l_ctx),
                  pl.BlockSpec((1, NA_HEADS_PER_STEP, r_blk * GRID_W, kr_n * GRID_W), bias_map)],
        out_specs=pl.BlockSpec((1, r_blk * GRID_W, w), lambda bi, hg, i: (bi, i, hg)),
        out_shape=jax.ShapeDtypeStruct((b, t, NA_WIDTH), BF16),
        compiler_params=pltpu.CompilerParams(dimension_semantics=("arbitrary",) * 3, vmem_limit_bytes=VMEM_LIMIT),
        name="na",
    )(q, k, v, kc, vc, bias)


def _rw_pre_kernel(p_ref, w0_ref, wup_hi_ref, wup_lo_ref, a0_ref, aup_ref,
                   gup_ref, kkw_ref, ka_ref, rk_ref, tri_ref, bd_ref,
                   g_ref, bon_ref, rh_ref, y0_ref, m_ref, n_ref):
    cl = CHUNK
    ps = p_ref[0]
    tm = ps.shape[0]

    r = ps[:, 0:RW_WIDTH]
    k = ps[:, RW_WIDTH:2 * RW_WIDTH]
    v = ps[:, 2 * RW_WIDTH:3 * RW_WIDTH]
    wd = ps[:, OFF_WD:OFF_WD + LANE]
    ad = ps[:, OFF_AD:OFF_AD + LANE]
    gd = ps[:, OFF_GD:OFF_GD + LANE]

    lora_w = _dot_hl3(jnp.tanh(wd), wup_hi_ref[...], wup_lo_ref[...])
    lora_a = _bdot(ad, aup_ref[...])
    g_ref[0] = _bdot(_sigmoid(gd), gup_ref[...])

    bd_mask = bd_ref[...]
    kkv = k * kkw_ref[...]
    kk = kkv * lax.rsqrt(jnp.maximum(_head_sum(kkv * kkv, bd_mask, split=False), 1e-24))

    grp = 4 * HEAD_DIM
    n_grp = RW_WIDTH // grp
    ri = lax.broadcasted_iota(jnp.int32, (cl, grp), 0)
    ci = lax.broadcasted_iota(jnp.int32, (cl, grp), 1) & (HEAD_DIM - 1)
    eye_f = (ri == ci).astype(F32)
    n_sq = int(np.log2(INV_BASE)) - 1
    blk = []
    size = INV_BASE
    while size <= cl:
        sh = int(np.log2(size))
        blk.append((ri >> sh) == (ci >> sh))
        size *= 2
    strict = ((ci < ri), (ci > ri))
    incl = ((ci <= ri), (ci >= ri))

    def bd(x):
        xb = x.astype(BF16)
        return jnp.concatenate([xb] * (grp // cl), axis=0) * bd_mask

    n_ch = tm // cl
    csl = lambda ch: slice(ch * cl, (ch + 1) * cl)
    bonus_terms = jnp.zeros((tm, RW_WIDTH), F32)
    per_dir = []
    for d in range(2):
        sl_d = slice(d * RW_WIDTH, (d + 1) * RW_WIDTH)
        z = w0_ref[d:d + 1, :] + lora_w[:, sl_d]
        lw = _sigmoid(z) * (-float(np.exp(-0.5)))
        a = _sigmoid(a0_ref[d:d + 1, :] + lora_a[:, sl_d])
        kd = k * (1.0 + (a - 1.0) * ka_ref[...])
        bvec = kk * a
        bonus_terms = bonus_terms + r * kd * rk_ref[...]

        l_hi, l_lo = _split2(lw)
        tri = tri_ref[d]
        cum = _dot(tri, l_hi) + _dot(tri, l_lo)
        tot = [jnp.sum(lw[csl(ch)], axis=0, keepdims=True) for ch in range(n_ch)]
        tot_rows = jnp.concatenate([jnp.broadcast_to(tc, (cl, RW_WIDTH)) for tc in tot], axis=0)
        e_neg = jnp.exp(-cum)
        e_tot = jnp.exp(tot_rows - cum)
        per_dir.append(dict(at=(-kk * jnp.exp(cum - lw)).astype(BF16), rt=r * jnp.exp(cum),
                            bt=(bvec * e_neg).astype(BF16), kt=(kd * e_neg).astype(BF16),
                            bb=(bvec * e_tot).astype(BF16), kb=(kd * e_tot).astype(BF16),
                            p_tot=[jnp.exp(tc) for tc in tot]))
    bon_ref[0] = _head_sum(bonus_terms, bd_mask, split=False) * v
    v_bf = v.astype(BF16)

    probs = [(ch, d, g) for ch in range(n_ch) for d in range(2) for g in range(n_grp)]
    rng = range(len(probs))
    gsl = lambda g: slice(g * grp, (g + 1) * grp)
    at = [per_dir[d]["at"][csl(ch), gsl(g)] for ch, d, g in probs]
    rt = [per_dir[d]["rt"][csl(ch), gsl(g)] for ch, d, g in probs]
    bd_v = {(ch, g): bd(v_bf[csl(ch), gsl(g)]) for ch in range(n_ch) for g in range(n_grp)}
    lhs = [jnp.concatenate([at[i], rt[i].astype(BF16)], axis=0) for i in rng]
    s_b = [_bdot_nt(lhs[i], bd(per_dir[d]["bt"][csl(ch), gsl(g)])) for i, (ch, d, g) in enumerate(probs)]
    s_k = [_bdot_nt(lhs[i], bd(per_dir[d]["kt"][csl(ch), gsl(g)])) for i, (ch, d, g) in enumerate(probs)]
    aab = [jnp.where(strict[probs[i][1]], s_b[i][:cl], 0.0) for i in rng]
    arb = [jnp.where(incl[probs[i][1]], s_b[i][cl:], 0.0).astype(BF16) for i in rng]
    aak = [jnp.where(strict[probs[i][1]], s_k[i][:cl], 0.0).astype(BF16) for i in rng]
    ark = [jnp.where(incl[probs[i][1]], s_k[i][cl:], 0.0).astype(BF16) for i in rng]
    xin = [jnp.where(blk[0], aab[i], 0.0) for i in rng]
    apow = [_bdot(xin[i], bd(xin[i])) for i in rng]
    for _ in range(n_sq - 1):
        res = [_bdot(jnp.concatenate([xin[i], apow[i]], axis=0), bd(apow[i])) for i in rng]
        xin = [xin[i] + (apow[i] + res[i][:cl]) for i in rng]
        apow = [res[i][cl:] for i in rng]
    xin = [xin[i] + (apow[i] + _bdot(xin[i], bd(apow[i]))) for i in rng]
    for lvl in range(1, len(blk)):
        sel = jnp.logical_and(blk[lvl], jnp.logical_not(blk[lvl - 1]))
        off = [jnp.where(sel, aab[i], 0.0) for i in rng]
        tmp = [off[i] + _bdot(off[i], bd(xin[i])) for i in rng]
        xin = [xin[i] + (tmp[i] + _bdot(xin[i], bd(tmp[i]))) for i in rng]
    xin = [xin[i].astype(BF16) for i in rng]
    ah = [(at[i].astype(F32) + _bdot(xin[i], bd(at[i]))).astype(BF16) for i in rng]
    av = [_bdot(aak[i], bd_v[probs[i][0], probs[i][2]]) for i in rng]
    wh = [(av[i] + _bdot(xin[i], bd(av[i]))).astype(BF16) for i in rng]
    for i, (ch, d, g) in enumerate(probs):
        rh_ref[d, 0, csl(ch), gsl(g)] = (rt[i] + _bdot(arb[i], bd(ah[i]))).astype(rh_ref.dtype)
    for i, (ch, d, g) in enumerate(probs):
        y0_ref[d, 0, csl(ch), gsl(g)] = (_bdot(arb[i], bd(wh[i])) + _bdot(ark[i], bd_v[ch, g])).astype(y0_ref.dtype)
    r2 = lax.broadcasted_iota(jnp.int32, (grp, grp), 0) >> int(np.log2(HEAD_DIM))
    c2 = lax.broadcasted_iota(jnp.int32, (grp, grp), 1) >> int(np.log2(HEAD_DIM))
    same_head = r2 == c2

    def fold(full):
        fm = jnp.where(same_head, full, 0.0)
        out = fm[0:HEAD_DIM]
        for j in range(1, grp // HEAD_DIM):
            out = out + fm[j * HEAD_DIM:(j + 1) * HEAD_DIM]
        return out

    for i, (ch, d, g) in enumerate(probs):
        bb = per_dir[d]["bb"][csl(ch), gsl(g)]
        diag = eye_f * per_dir[d]["p_tot"][ch][:, gsl(g)]
        m_ref[d, 0, ch, :, gsl(g)] = (diag + fold(_bdot_tn(bb, ah[i]))).astype(m_ref.dtype)
    for i, (ch, d, g) in enumerate(probs):
        bk = jnp.concatenate([per_dir[d]["bb"][csl(ch), gsl(g)], per_dir[d]["kb"][csl(ch), gsl(g)]], axis=0)
        wv = jnp.concatenate([wh[i], v_bf[csl(ch), gsl(g)]], axis=0)
        n_ref[d, 0, ch, :, gsl(g)] = fold(_bdot_tn(bk, wv)).astype(n_ref.dtype)


def _rw_pre(prw, consts):
    b, t, _ = prw.shape
    cl = CHUNK
    cps = RW_CHUNKS_PER_STEP
    tm = cps * cl
    nc = t // cl
    assert t % tm == 0 and tm % 8 == 0
    tok = lambda w: pl.BlockSpec((1, tm, w), lambda bi, c: (bi, c, 0))
    in_specs = [tok(RW_COLS)] + [_const_spec(a.shape) for a in consts]
    dirtok = pl.BlockSpec((2, 1, tm, RW_WIDTH), lambda bi, c: (0, bi, c, 0))
    mat = pl.BlockSpec((2, 1, cps, HEAD_DIM, RW_WIDTH), lambda bi, c: (0, bi, c, 0, 0))
    return pl.pallas_call(
        _rw_pre_kernel,
        grid=(b, t // tm),
        in_specs=in_specs,
        out_specs=[tok(RW_WIDTH), tok(RW_WIDTH), dirtok, dirtok, mat, mat],
        out_shape=[jax.ShapeDtypeStruct((b, t, RW_WIDTH), F32)] * 2
        + [jax.ShapeDtypeStruct((2, b, t, RW_WIDTH), BF16)] * 2
        + [jax.ShapeDtypeStruct((2, b, nc, HEAD_DIM, RW_WIDTH), BF16)] * 2,
        compiler_params=pltpu.CompilerParams(dimension_semantics=("arbitrary", "arbitrary"),
                                             vmem_limit_bytes=VMEM_LIMIT),
        name="rw_pre",
    )(prw, *consts)


def _rw_scan_kernel(rhf_ref, y0f_ref, mf_ref, nf_ref, rhr_ref, y0r_ref, mr_ref, nr_ref, z0_ref, bd_ref,
                    yf_ref, yr_ref, zf_ref, z_scr):
    j = pl.program_id(0)

    @pl.when(j == 0)
    def _():
        z_scr[...] = z0_ref[...]

    cl = CHUNK
    nb = z_scr.shape[1]
    n_ch = rhf_ref.shape[2] // cl
    bd_mask = bd_ref[...]
    grp = bd_mask.shape[0]
    dirs = ((rhf_ref, y0f_ref, mf_ref, nf_ref, yf_ref), (rhr_ref, y0r_ref, mr_ref, nr_ref, yr_ref))
    for step in range(n_ch):
        for d, (rh_ref, y0_ref, m_ref, n_ref, y_ref) in enumerate(dirs):
            ch = step if d == 0 else n_ch - 1 - step
            rows = slice(ch * cl, (ch + 1) * cl)
            for bi in range(nb):
                for g in range(RW_WIDTH // grp):
                    gs = slice(g * grp, (g + 1) * grp)
                    z_bd = jnp.concatenate([z_scr[d, bi, :, gs].astype(BF16)] * (grp // HEAD_DIM), axis=0) * bd_mask
                    lhs = jnp.concatenate([rh_ref[0, bi, rows, gs], m_ref[0, bi, ch, :, gs]], axis=0)
                    res = _dot(lhs, z_bd)
                    y_ref[bi, rows, gs] = res[:cl] + y0_ref[0, bi, rows, gs]
                    z_scr[d, bi, :, gs] = res[cl:] + n_ref[0, bi, ch, :, gs]

    @pl.when(j == pl.num_programs(0) - 1)
    def _():
        zf_ref[...] = z_scr[...]


def _rw_scan(rh, y0, m, n, z0, bd_mask):
    _, b, t, _ = rh.shape
    cl = CHUNK
    cps = RW_SCAN_CHUNKS_PER_STEP
    tm = cps * cl
    ns = t // tm
    assert t % tm == 0
    tokf = pl.BlockSpec((1, b, tm, RW_WIDTH), lambda j: (0, 0, j, 0))
    tokr = pl.BlockSpec((1, b, tm, RW_WIDTH), lambda j: (1, 0, ns - 1 - j, 0))
    matf = pl.BlockSpec((1, b, cps, HEAD_DIM, RW_WIDTH), lambda j: (0, 0, j, 0, 0))
    matr = pl.BlockSpec((1, b, cps, HEAD_DIM, RW_WIDTH), lambda j: (1, 0, ns - 1 - j, 0, 0))
    zspec = _const_spec(z0.shape)
    return pl.pallas_call(
        _rw_scan_kernel,
        grid=(ns,),
        in_specs=[tokf, tokf, matf, matf, tokr, tokr, matr, matr, zspec, _const_spec(bd_mask.shape)],
        out_specs=[pl.BlockSpec((b, tm, RW_WIDTH), lambda j: (0, j, 0)),
                   pl.BlockSpec((b, tm, RW_WIDTH), lambda j: (0, ns - 1 - j, 0)),
                   zspec],
        out_shape=[jax.ShapeDtypeStruct((b, t, RW_WIDTH), F32)] * 2 + [jax.ShapeDtypeStruct(z0.shape, F32)],
        scratch_shapes=[pltpu.VMEM(z0.shape, F32)],
        compiler_params=pltpu.CompilerParams(dimension_semantics=("arbitrary",), vmem_limit_bytes=VMEM_LIMIT),
        name="rw_scan",
    )(rh, y0, m, n, rh, y0, m, n, z0, bd_mask)


def _tail_kernel(yf_ref, yr_ref, bon_ref, g_ref, ona_ref, x_ref, gate1_ref, sh_ref, sc_ref, gate2_ref,
                 lng_ref, lnb_ref, ones_ref, wona_ref, worw_ref, g2_ref, w1g_ref, w1u_ref, w2_ref, o_ref, *, ff_chunk):
    ones = ones_ref[...]
    wkv = yf_ref[0] + yr_ref[0]
    mu = _head_sum(wkv, ones, split=False) * (1.0 / HEAD_DIM)
    dlt = wkv - mu
    var = _head_sum(dlt * dlt, ones, split=False) * (1.0 / HEAD_DIM)
    yn = dlt * lax.rsqrt(var + RW_GN_EPS) * lng_ref[...] + lnb_ref[...]
    orw = ((yn + bon_ref[0]) * g_ref[0]).astype(BF16)
    mix = _dot(ona_ref[0], wona_ref[...]) + _dot(orw, worw_ref[...])
    h1 = x_ref[0] + gate1_ref[0] * mix
    ms = jnp.mean(h1 * h1, axis=-1, keepdims=True)
    u = (h1 * lax.rsqrt(ms + NORM_EPS) * g2_ref[...] * (1.0 + sc_ref[0]) + sh_ref[0]).astype(BF16)
    bounds = _ff_bounds(w2_ref.shape[0], ff_chunk)

    def up_proj(lo, hi):
        return _dot(u, w1g_ref[:, lo:hi]), _dot(u, w1u_ref[:, lo:hi])

    acc = jnp.zeros(h1.shape, F32)
    nxt = up_proj(*bounds[0])
    for ci, (lo, hi) in enumerate(bounds):
        gt, up = nxt
        if ci + 1 < len(bounds):
            nxt = up_proj(*bounds[ci + 1])
        act = (gt * _sigmoid(gt) * up).astype(BF16)
        acc = acc + _dot(act, w2_ref[lo:hi, :])
    o_ref[0] = h1 + gate2_ref[0] * acc


FF_CHUNK = 512


def _ff_bounds(d_ff, ff_chunk):
    edges = list(range(0, d_ff, ff_chunk)) + [d_ff]
    return [(lo, hi) for lo, hi in zip(edges[:-1], edges[1:])]


def _tail(yf, yr, bon, g, ona, x, gate1, sh, sc, gate2, lng, lnb, ones_bd, wona, worw, g2, w1g, w1u, w2):
    b, t, d = x.shape
    tm = min(512, t)
    vec = pl.BlockSpec((1, 1, d), lambda bi, i: (bi, 0, 0))
    tok = lambda w: pl.BlockSpec((1, tm, w), lambda bi, i: (bi, i, 0))
    consts = (lng, lnb, ones_bd, wona, worw, g2, w1g, w1u, w2)
    return pl.pallas_call(
        functools.partial(_tail_kernel, ff_chunk=FF_CHUNK),
        grid=(b, t // tm),
        in_specs=[tok(RW_WIDTH)] * 4 + [tok(NA_WIDTH), tok(d), vec, vec, vec, vec]
        + [_const_spec(a.shape, single=True) for a in consts],
        out_specs=tok(d),
        out_shape=jax.ShapeDtypeStruct((b, t, d), F32),
        compiler_params=pltpu.CompilerParams(dimension_semantics=("arbitrary", "arbitrary"),
                                             vmem_limit_bytes=VMEM_LIMIT),
        name="tail",
    )(yf, yr, bon, g, ona, x, gate1, sh, sc, gate2, *consts)


def _pad_cols(w, width):
    return jnp.pad(w, ((0, 0), (0, width - w.shape[1])))


def _rw_layout(w):
    o = 3 * RW_WIDTH
    wd = w[:, o:o + 2 * DECAY_LORA]
    ad = _pad_cols(w[:, o + 2 * DECAY_LORA:o + 2 * DECAY_LORA + AAA_LORA], LANE)
    gd = w[:, o + 2 * DECAY_LORA + AAA_LORA:]
    return jnp.concatenate([w[:, :o], wd, ad, gd], axis=1)


def _hl(w):
    hi = w.astype(BF16)
    return hi, (w - hi.astype(F32)).astype(BF16)


def kernel(x, c, ctx, c_ctx, norm1_g, norm2_g, w_ada, b_ada, w_in, na_q_g, na_k_g, na_rpb, rw_mu_prev, rw_mu_next,
           rw_w0, rw_w_up, rw_a0, rw_a_up, rw_g_up, rw_k_k, rw_k_a, rw_r_k, rw_ln_g, rw_ln_b, w_out, ffn_w_in,
           ffn_w_out):
    depth = w_in.shape[0]
    assert depth == 1, "single-layer kernel"
    b, t, d = x.shape
    l_ctx = ctx.shape[1]
    lyr = 0

    n_rows = -(-(b + 1) // 8) * 8
    c_rows = jnp.zeros((n_rows, d), F32).at[:b].set(c).at[b].set(c_ctx)
    mod_all = _ada(c_rows, w_ada[lyr], b_ada[lyr][None, :])
    mod = [mod_all[:b, i * d:(i + 1) * d][:, None, :] for i in range(6)]
    modc = [jnp.broadcast_to(mod_all[b, i * d:(i + 1) * d][None, None, :], (b, 1, d)) for i in range(2)]

    w_na = w_in[lyr][:, :3 * NA_WIDTH].astype(BF16)
    w_rw = _rw_layout(w_in[lyr][:, 3 * NA_WIDTH:]).astype(BF16)
    hd = np.arange(4 * HEAD_DIM) // HEAD_DIM
    ones_bd = jnp.asarray(hd[:, None] == hd[None, :], BF16)
    qg = jnp.tile(na_q_g[lyr], NA_HEADS)[None, :]
    kg = jnp.tile(na_k_g[lyr], NA_HEADS)[None, :]
    g1 = norm1_g[lyr][None, :]

    mu_p = _rw_layout(rw_mu_prev[lyr][None, :])
    mu_n = _rw_layout(rw_mu_next[lyr][None, :])
    q, k, v, prw = _inproj(x, mod[0], mod[1], g1, w_na, w_rw, qg, kg, ones_bd, mu_p, mu_n)
    _, kc, vc, prw_c = _inproj(ctx, modc[0], modc[1], g1, w_na, w_rw, qg, kg, ones_bd, mu_p, mu_n)

    bias = _na_bias_tables(na_rpb[lyr], t // GRID_W, NA_ROWS_PER_STEP)
    o_na = _na(q, k, v, kc, vc, bias)

    zeros_up = jnp.zeros((DECAY_LORA, RW_WIDTH), F32)
    wup = jnp.concatenate([jnp.concatenate([rw_w_up[lyr, 0], zeros_up], axis=1),
                           jnp.concatenate([zeros_up, rw_w_up[lyr, 1]], axis=1)], axis=0)
    aup = jnp.concatenate([jnp.concatenate([rw_a_up[lyr, 0], rw_a_up[lyr, 1]], axis=1),
                           jnp.zeros((LANE - AAA_LORA, 2 * RW_WIDTH), F32)], axis=0)
    ri = np.arange(RW_CHUNKS_PER_STEP * CHUNK)
    same = (ri[None, :] // CHUNK) == (ri[:, None] // CHUNK)
    tri = jnp.asarray(np.stack([same & (ri[None, :] <= ri[:, None]), same & (ri[None, :] >= ri[:, None])]), BF16)
    consts = (rw_w0[lyr], *_hl(wup), rw_a0[lyr], aup.astype(BF16), rw_g_up[lyr].astype(BF16),
              rw_k_k[lyr][None, :], rw_k_a[lyr][None, :], rw_r_k[lyr].reshape(1, RW_WIDTH), tri, ones_bd)

    _, _, rh_c, y0_c, m_c, n_c = _rw_pre(prw_c, consts)
    g, bon, rh, y0, m, n = _rw_pre(prw, consts)
    z0 = jnp.zeros((2, b, HEAD_DIM, RW_WIDTH), F32)
    _, _, z_ctx = _rw_scan(rh_c, y0_c, m_c, n_c, z0, ones_bd)
    yf, yr, _ = _rw_scan(rh, y0, m, n, z_ctx, ones_bd)

    wo = w_out[lyr].astype(BF16)
    d_ff = ffn_w_out.shape[1]
    w1 = ffn_w_in[lyr].astype(BF16)
    return _tail(yf, yr, bon, g, o_na, x, mod[2], mod[3], mod[4], mod[5],
                 rw_ln_g[lyr][None, :], rw_ln_b[lyr][None, :], ones_bd, wo[:NA_WIDTH], wo[NA_WIDTH:],
                 norm2_g[lyr][None, :], w1[:, :d_ff], w1[:, d_ff:], ffn_w_out[lyr].astype(BF16))
```

```python
import functools

import numpy as np
import jax
import jax.numpy as jnp
from jax import lax
from jax.experimental import pallas as pl
from jax.experimental.pallas import tpu as pltpu

F32 = jnp.float32
BF16 = jnp.bfloat16

GRID_W = 64
NA_HEADS = 8
HEAD_DIM = 64
NA_WIDTH = NA_HEADS * HEAD_DIM
NA_WIN_ROWS = 8
NA_WIN_COLS = 16
RW_HEADS = 8
RW_WIDTH = RW_HEADS * HEAD_DIM
DECAY_LORA = 64
AAA_LORA = 64
GATE_LORA = 128
NORM_EPS = 1e-6
RW_GN_EPS = 64e-5

LANE = 128
RW_COLS = 3 * RW_WIDTH + 3 * LANE
OFF_WD = 3 * RW_WIDTH
OFF_AD = OFF_WD + LANE
OFF_GD = OFF_AD + LANE
CHUNK = 64
INV_BASE = 16
RW_SCAN_CHUNKS_PER_STEP = 4
RW_CHUNKS_PER_STEP = 4
NA_ROWS_PER_STEP = 4
NA_HEADS_PER_STEP = 4
NEG_BIG = -1e30
VMEM_LIMIT = 56 * 1024 * 1024


def _dot(a, b):
    return jnp.dot(a, b, preferred_element_type=F32)


def _bdot(a, b):
    return jnp.dot(a.astype(BF16), b.astype(BF16), preferred_element_type=F32)


def _bdot_nt(a, b):
    return lax.dot_general(a.astype(BF16), b.astype(BF16), (((1,), (1,)), ((), ())),
                           preferred_element_type=F32)


def _bdot_tn(a, b):
    return lax.dot_general(a.astype(BF16), b.astype(BF16), (((0,), (0,)), ((), ())),
                           preferred_element_type=F32)


def _split2(a):
    hi = a.astype(BF16)
    lo = (a - hi.astype(F32)).astype(BF16)
    return hi, lo


def _head_sum(x, ones_grp, split=True):
    m, w = x.shape
    grp = ones_grp.shape[0]
    n = w // grp
    pieces = _split2(x) if split else (x.astype(BF16),)
    parts = [t[:, g * grp:(g + 1) * grp] for t in pieces for g in range(n)]
    res = _dot(jnp.concatenate(parts, axis=0), ones_grp)
    cols = []
    for g in range(n):
        col = res[g * m:(g + 1) * m]
        if split:
            col = col + res[(n + g) * m:(n + g + 1) * m]
        cols.append(col)
    return jnp.concatenate(cols, axis=1)


def _dot_hl3(a, w_hi, w_lo):
    m = a.shape[0]
    hi, lo = _split2(a)
    res = _dot(jnp.concatenate([hi, lo], axis=0), w_hi)
    return res[:m] + (res[m:] + _dot(hi, w_lo))


def _sigmoid(x):
    return jax.nn.sigmoid(x)


def _const_spec(shape, single=False):
    nd = len(shape)
    if single:
        return pl.BlockSpec(shape, lambda *_: (0,) * nd, pipeline_mode=pl.Buffered(1))
    return pl.BlockSpec(shape, lambda *_: (0,) * nd)


def _ada_kernel(c_ref, w_ref, b_ref, o_ref):
    c = c_ref[...]
    s = c * _sigmoid(c)
    w_hi, w_lo = _split2(w_ref[...])
    o_ref[...] = _dot_hl3(s, w_hi, w_lo) + b_ref[...]


def _ada(c_rows, w_ada, b_ada):
    rows, d = c_rows.shape
    n = w_ada.shape[1]
    tn = 1024
    return pl.pallas_call(
        _ada_kernel,
        grid=(n // tn,),
        in_specs=[pl.BlockSpec((rows, d), lambda j: (0, 0)),
                  pl.BlockSpec((d, tn), lambda j: (0, j)),
                  pl.BlockSpec((1, tn), lambda j: (0, j))],
        out_specs=pl.BlockSpec((rows, tn), lambda j: (0, j)),
        out_shape=jax.ShapeDtypeStruct((rows, n), F32),
        compiler_params=pltpu.CompilerParams(dimension_semantics=("arbitrary",), vmem_limit_bytes=VMEM_LIMIT),
        name="ada",
    )(c_rows, w_ada, b_ada)


def _inproj_kernel(x_ref, xp_ref, xn_ref, sh_ref, sc_ref, g_ref, wna_ref, wrw_ref, qg_ref, kg_ref, ones_ref,
                   mup_ref, mun_ref, q_ref, k_ref, v_ref, prw_ref):
    i = pl.program_id(1)

    def modulated(xt):
        ms = jnp.mean(xt * xt, axis=-1, keepdims=True)
        y = xt * lax.rsqrt(ms + NORM_EPS) * g_ref[...]
        return (y * (1.0 + sc_ref[0]) + sh_ref[0]).astype(BF16)

    u = modulated(x_ref[0])

    tm = u.shape[0]
    u_halo = modulated(jnp.concatenate([xp_ref[0], xn_ref[0]], axis=0))
    p_all = _dot(jnp.concatenate([u, u_halo], axis=0), wrw_ref[...])
    p = p_all[:tm]
    ph = p_all[tm:]
    halo_prev = ph[7:8] * (i > 0).astype(F32)
    halo_next = ph[8:9] * (i < pl.num_programs(1) - 1).astype(F32)
    row = lax.broadcasted_iota(jnp.int32, (tm, 1), 0)
    prev = jnp.where(row == 0, halo_prev, pltpu.roll(p, 1, 0))
    nxt = jnp.where(row == tm - 1, halo_next, pltpu.roll(p, tm - 1, 0))
    prw_ref[0] = p + mup_ref[...] * (prev - p) + mun_ref[...] * (nxt - p)

    pn = _dot(u, wna_ref[...])

    def head_norm(t, gain):
        ss = _head_sum(t * t, ones_ref[...], split=False) * (1.0 / HEAD_DIM)
        return t * lax.rsqrt(ss + NORM_EPS) * gain

    q = head_norm(pn[:, :NA_WIDTH], qg_ref[...]) * (HEAD_DIM ** -0.5)
    k = head_norm(pn[:, NA_WIDTH:2 * NA_WIDTH], kg_ref[...])
    q_ref[0] = q.astype(BF16)
    k_ref[0] = k.astype(BF16)
    v_ref[0] = pn[:, 2 * NA_WIDTH:].astype(BF16)


def _inproj(x, shift, scale, g1, w_na, w_rw, qg, kg, ones_bd, mu_p, mu_n):
    b, t, d = x.shape
    tm = min(512, t)
    sub = tm // 8
    nb8 = t // 8
    vec = pl.BlockSpec((1, 1, d), lambda bi, i: (bi, 0, 0))
    tok = lambda w: pl.BlockSpec((1, tm, w), lambda bi, i: (bi, i, 0))
    return pl.pallas_call(
        _inproj_kernel,
        grid=(b, t // tm),
        in_specs=[tok(d),
                  pl.BlockSpec((1, 8, d), lambda bi, i: (bi, jnp.maximum(i * sub - 1, 0), 0)),
                  pl.BlockSpec((1, 8, d), lambda bi, i: (bi, jnp.minimum((i + 1) * sub, nb8 - 1), 0)),
                  vec, vec, _const_spec((1, d)), _const_spec(w_na.shape), _const_spec(w_rw.shape),
                  _const_spec((1, NA_WIDTH)), _const_spec((1, NA_WIDTH)), _const_spec(ones_bd.shape),
                  _const_spec(mu_p.shape), _const_spec(mu_n.shape)],
        out_specs=[tok(NA_WIDTH), tok(NA_WIDTH), tok(NA_WIDTH), tok(RW_COLS)],
        out_shape=[jax.ShapeDtypeStruct((b, t, NA_WIDTH), BF16)] * 3
        + [jax.ShapeDtypeStruct((b, t, RW_COLS), F32)],
        compiler_params=pltpu.CompilerParams(dimension_semantics=("arbitrary", "arbitrary"),
                                             vmem_limit_bytes=VMEM_LIMIT),
        name="inproj",
    )(x, x, x, shift, scale, g1, w_na, w_rw, qg, kg, ones_bd, mu_p, mu_n)


def _na_bias_tables(rpb, rows, r_blk):
    kr_n = r_blk + NA_WIN_ROWS - 1
    nblk = rows // r_blk
    n_h, n_di, n_dj = rpb.shape
    w = GRID_W
    lo = w - NA_WIN_COLS
    strip = jnp.pad(rpb, ((0, 0), (0, 0), (lo, 2 * w - 1 - n_dj - lo)))
    skew = jnp.tile(strip, (1, 1, w + 1))[:, :, :2 * w * w].reshape(n_h, n_di, w, 2 * w)[..., :w]
    toe = skew[:, :, ::-1, :]
    j = np.arange(w)
    cs = np.clip(j - NA_WIN_COLS // 2, 0, w - NA_WIN_COLS)
    c = np.arange(w)
    cvalid = (c[None, :] >= cs[:, None]) & (c[None, :] < cs[:, None] + NA_WIN_COLS)
    toe = jnp.where(cvalid[None, None], toe, NEG_BIG)
    neg = jnp.full((n_h, w, w), NEG_BIG, F32)
    tabs = []
    for ib in (0, min(1, nblk - 1), nblk - 1):
        i0 = ib * r_blk
        base = int(np.clip(i0 - NA_WIN_ROWS // 2, 0, rows - kr_n))
        q_rows = []
        for i in range(i0, i0 + r_blk):
            rs = int(np.clip(i - NA_WIN_ROWS // 2, 0, rows - NA_WIN_ROWS))
            blocks = []
            for kr in range(base, base + kr_n):
                blocks.append(toe[:, kr - i + NA_WIN_ROWS - 1] if rs <= kr < rs + NA_WIN_ROWS else neg)
            q_rows.append(jnp.concatenate(blocks, axis=-1))
        tabs.append(jnp.concatenate(q_rows, axis=1))
    return jnp.stack(tabs)


def _na_kernel(q_ref, k_ref, v_ref, kc_ref, vc_ref, bias_ref, o_ref, *, r_blk, rows):
    i = pl.program_id(2)
    kr_n = r_blk + NA_WIN_ROWS - 1
    base = jnp.clip(i * r_blk - NA_WIN_ROWS // 2, 0, rows - kr_n)
    start = pl.multiple_of(base * GRID_W, GRID_W)
    n_pair = q_ref.shape[2] // LANE
    nq = q_ref.shape[1]
    lane = lax.broadcasted_iota(jnp.int32, (nq, LANE), 1)
    psl = lambda pr: slice(pr * LANE, (pr + 1) * LANE)
    s, sc = [], []
    for pr in range(n_pair):
        q = q_ref[0, :, psl(pr)]
        q2 = jnp.concatenate([jnp.where(lane < HEAD_DIM, q, jnp.zeros_like(q)),
                              jnp.where(lane >= HEAD_DIM, q, jnp.zeros_like(q))], axis=0)
        bias2 = jnp.concatenate([bias_ref[0, 2 * pr], bias_ref[0, 2 * pr + 1]], axis=0)
        s.append(_bdot_nt(q2, k_ref[0, pl.ds(start, kr_n * GRID_W), psl(pr)]) + bias2)
        sc.append(_bdot_nt(q2, kc_ref[0, :, psl(pr)]))
    for pr in range(n_pair):
        m = jnp.maximum(jnp.max(s[pr], axis=-1, keepdims=True), jnp.max(sc[pr], axis=-1, keepdims=True))
        p = jnp.exp(s[pr] - m)
        pc = jnp.exp(sc[pr] - m)
        l = jnp.sum(p, axis=-1, keepdims=True) + jnp.sum(pc, axis=-1, keepdims=True)
        o = _bdot(p, v_ref[0, pl.ds(start, kr_n * GRID_W), psl(pr)]) + _bdot(pc, vc_ref[0, :, psl(pr)])
        o = o / l
        o_ref[0, :, psl(pr)] = jnp.where(lane < HEAD_DIM, o[:nq], o[nq:]).astype(o_ref.dtype)


def _na(q, k, v, kc, vc, bias):
    b, t, _ = q.shape
    l_ctx = kc.shape[1]
    rows = t // GRID_W
    r_blk = NA_ROWS_PER_STEP
    kr_n = r_blk + NA_WIN_ROWS - 1
    nblk = rows // r_blk
    assert rows % r_blk == 0 and rows >= kr_n and t % GRID_W == 0
    w = NA_HEADS_PER_STEP * HEAD_DIM
    n_grp = NA_WIDTH // w

    def bias_map(bi, hg, i):
        case = jnp.where(i == 0, 0, jnp.where(i == nblk - 1, 2, 1))
        return (case, hg, 0, 0)

    whole = lambda rows_: pl.BlockSpec((1, rows_, w), lambda bi, hg, i: (bi, 0, hg))
    return pl.pallas_call(
        functools.partial(_na_kernel, r_blk=r_blk, rows=rows),
        grid=(b, n_grp, nblk),
        in_specs=[pl.BlockSpec((1, r_blk * GRID_W, w), lambda bi, hg, i: (bi, i, hg)),
                  whole(t), whole(t), whole(l_ctx), whole(l_ctx),
                  pl.BlockSpec((1, NA_HEADS_PER_STEP, r_blk * GRID_W, kr_n * GRID_W), bias_map)],
        out_specs=pl.BlockSpec((1, r_blk * GRID_W, w), lambda bi, hg, i: (bi, i, hg)),
        out_shape=jax.ShapeDtypeStruct((b, t, NA_WIDTH), BF16),
        compiler_params=pltpu.CompilerParams(dimension_semantics=("arbitrary",) * 3, vmem_limit_bytes=VMEM_LIMIT),
        name="na",
    )(q, k, v, kc, vc, bias)


def _rw_pre_kernel(p_ref, w0_ref, wup_hi_ref, wup_lo_ref, a0_ref, aup_ref,
                   gup_ref, kkw_ref, ka_ref, rk_ref, tri_ref, bd_ref,
                   g_ref, bon_ref, rh_ref, y0_ref, m_ref, n_ref):
    cl = CHUNK
    ps = p_ref[0]
    tm = ps.shape[0]

    r = ps[:, 0:RW_WIDTH]
    k = ps[:, RW_WIDTH:2 * RW_WIDTH]
    v = ps[:, 2 * RW_WIDTH:3 * RW_WIDTH]
    wd = ps[:, OFF_WD:OFF_WD + LANE]
    ad = ps[:, OFF_AD:OFF_AD + LANE]
    gd = ps[:, OFF_GD:OFF_GD + LANE]

    lora_w = _dot_hl3(jnp.tanh(wd), wup_hi_ref[...], wup_lo_ref[...])
    lora_a = _bdot(ad, aup_ref[...])
    g_ref[0] = _bdot(_sigmoid(gd), gup_ref[...]).astype(g_ref.dtype)

    bd_mask = bd_ref[...]
    kkv = k * kkw_ref[...]
    kk = kkv * lax.rsqrt(jnp.maximum(_head_sum(kkv * kkv, bd_mask, split=False), 1e-24))

    grp = 4 * HEAD_DIM
    n_grp = RW_WIDTH // grp
    ri = lax.broadcasted_iota(jnp.int32, (cl, grp), 0)
    ci = lax.broadcasted_iota(jnp.int32, (cl, grp), 1) & (HEAD_DIM - 1)
    eye_f = (ri == ci).astype(F32)
    n_sq = int(np.log2(INV_BASE)) - 1
    blk = []
    size = INV_BASE
    while size <= cl:
        sh = int(np.log2(size))
        blk.append((ri >> sh) == (ci >> sh))
        size *= 2
    strict = ((ci < ri), (ci > ri))
    incl = ((ci <= ri), (ci >= ri))

    def bd(x):
        xb = x.astype(BF16)
        return jnp.concatenate([xb] * (grp // cl), axis=0) * bd_mask

    n_ch = tm // cl
    csl = lambda ch: slice(ch * cl, (ch + 1) * cl)
    bonus_terms = jnp.zeros((tm, RW_WIDTH), F32)
    per_dir = []
    for d in range(2):
        sl_d = slice(d * RW_WIDTH, (d + 1) * RW_WIDTH)
        z = w0_ref[d:d + 1, :] + lora_w[:, sl_d]
        lw = _sigmoid(z) * (-float(np.exp(-0.5)))
        a = _sigmoid(a0_ref[d:d + 1, :] + lora_a[:, sl_d])
        kd = k * (1.0 + (a - 1.0) * ka_ref[...])
        bvec = kk * a
        bonus_terms = bonus_terms + r * kd * rk_ref[...]

        l_hi, l_lo = _split2(lw)
        tri = tri_ref[d]
        cum = _dot(tri, l_hi) + _dot(tri, l_lo)
        tot = [jnp.sum(lw[csl(ch)], axis=0, keepdims=True) for ch in range(n_ch)]
        tot_rows = jnp.concatenate([jnp.broadcast_to(tc, (cl, RW_WIDTH)) for tc in tot], axis=0)
        e_neg = jnp.exp(-cum)
        e_tot = jnp.exp(tot_rows - cum)
        per_dir.append(dict(at=(-kk * jnp.exp(cum - lw)).astype(BF16), rt=r * jnp.exp(cum),
                            bt=(bvec * e_neg).astype(BF16), kt=(kd * e_neg).astype(BF16),
                            bb=(bvec * e_tot).astype(BF16), kb=(kd * e_tot).astype(BF16),
                            p_tot=[jnp.exp(tc) for tc in tot]))
    bon_ref[0] = (_head_sum(bonus_terms, bd_mask, split=False) * v).astype(bon_ref.dtype)
    v_bf = v.astype(BF16)

    probs = [(ch, d, g) for ch in range(n_ch) for d in range(2) for g in range(n_grp)]
    rng = range(len(probs))
    gsl = lambda g: slice(g * grp, (g + 1) * grp)
    at = [per_dir[d]["at"][csl(ch), gsl(g)] for ch, d, g in probs]
    rt = [per_dir[d]["rt"][csl(ch), gsl(g)] for ch, d, g in probs]
    bd_v = {(ch, g): bd(v_bf[csl(ch), gsl(g)]) for ch in range(n_ch) for g in range(n_grp)}
    lhs = [jnp.concatenate([at[i], rt[i].astype(BF16)], axis=0) for i in rng]
    s_b = [_bdot_nt(lhs[i], bd(per_dir[d]["bt"][csl(ch), gsl(g)])) for i, (ch, d, g) in enumerate(probs)]
    s_k = [_bdot_nt(lhs[i], bd(per_dir[d]["kt"][csl(ch), gsl(g)])) for i, (ch, d, g) in enumerate(probs)]
    aab = [jnp.where(strict[probs[i][1]], s_b[i][:cl], 0.0) for i in rng]
    arb = [jnp.where(incl[probs[i][1]], s_b[i][cl:], 0.0).astype(BF16) for i in rng]
    aak = [jnp.where(strict[probs[i][1]], s_k[i][:cl], 0.0).astype(BF16) for i in rng]
    ark = [jnp.where(incl[probs[i][1]], s_k[i][cl:], 0.0).astype(BF16) for i in rng]
    xin = [jnp.where(blk[0], aab[i], 0.0) for i in rng]
    apow = [_bdot(xin[i], bd(xin[i])) for i in rng]
    for _ in range(n_sq - 1):
        res = [_bdot(jnp.concatenate([xin[i], apow[i]], axis=0), bd(apow[i])) for i in rng]
        xin = [xin[i] + (apow[i] + res[i][:cl]) for i in rng]
        apow = [res[i][cl:] for i in rng]
    xin = [xin[i] + (apow[i] + _bdot(xin[i], bd(apow[i]))) for i in rng]
    for lvl in range(1, len(blk)):
        sel = jnp.logical_and(blk[lvl], jnp.logical_not(blk[lvl - 1]))
        off = [jnp.where(sel, aab[i], 0.0) for i in rng]
        tmp = [off[i] + _bdot(off[i], bd(xin[i])) for i in rng]
        xin = [xin[i] + (tmp[i] + _bdot(xin[i], bd(tmp[i]))) for i in rng]
    xin = [xin[i].astype(BF16) for i in rng]
    ah = [(at[i].astype(F32) + _bdot(xin[i], bd(at[i]))).astype(BF16) for i in rng]
    av = [_bdot(aak[i], bd_v[probs[i][0], probs[i][2]]) for i in rng]
    wh = [(av[i] + _bdot(xin[i], bd(av[i]))).astype(BF16) for i in rng]
    for i, (ch, d, g) in enumerate(probs):
        rh_ref[d, 0, csl(ch), gsl(g)] = (rt[i] + _bdot(arb[i], bd(ah[i]))).astype(rh_ref.dtype)
    for i, (ch, d, g) in enumerate(probs):
        y0_ref[d, 0, csl(ch), gsl(g)] = (_bdot(arb[i], bd(wh[i])) + _bdot(ark[i], bd_v[ch, g])).astype(y0_ref.dtype)
    r2 = lax.broadcasted_iota(jnp.int32, (grp, grp), 0) >> int(np.log2(HEAD_DIM))
    c2 = lax.broadcasted_iota(jnp.int32, (grp, grp), 1) >> int(np.log2(HEAD_DIM))
    same_head = r2 == c2

    def fold(full):
        fm = jnp.where(same_head, full, 0.0)
        out = fm[0:HEAD_DIM]
        for j in range(1, grp // HEAD_DIM):
            out = out + fm[j * HEAD_DIM:(j + 1) * HEAD_DIM]
        return out

    for i, (ch, d, g) in enumerate(probs):
        bb = per_dir[d]["bb"][csl(ch), gsl(g)]
        diag = eye_f * per_dir[d]["p_tot"][ch][:, gsl(g)]
        m_ref[d, 0, ch, :, gsl(g)] = (diag + fold(_bdot_tn(bb, ah[i]))).astype(m_ref.dtype)
    for i, (ch, d, g) in enumerate(probs):
        bk = jnp.concatenate([per_dir[d]["bb"][csl(ch), gsl(g)], per_dir[d]["kb"][csl(ch), gsl(g)]], axis=0)
        wv = jnp.concatenate([wh[i], v_bf[csl(ch), gsl(g)]], axis=0)
        n_ref[d, 0, ch, :, gsl(g)] = fold(_bdot_tn(bk, wv)).astype(n_ref.dtype)


def _rw_pre(prw, consts):
    b, t, _ = prw.shape
    cl = CHUNK
    cps = RW_CHUNKS_PER_STEP
    tm = cps * cl
    nc = t // cl
    assert t % tm == 0 and tm % 8 == 0
    tok = lambda w: pl.BlockSpec((1, tm, w), lambda bi, c: (bi, c, 0))
    in_specs = [tok(RW_COLS)] + [_const_spec(a.shape) for a in consts]
    dirtok = pl.BlockSpec((2, 1, tm, RW_WIDTH), lambda bi, c: (0, bi, c, 0))
    mat = pl.BlockSpec((2, 1, cps, HEAD_DIM, RW_WIDTH), lambda bi, c: (0, bi, c, 0, 0))
    return pl.pallas_call(
        _rw_pre_kernel,
        grid=(b, t // tm),
        in_specs=in_specs,
        out_specs=[tok(RW_WIDTH), tok(RW_WIDTH), dirtok, dirtok, mat, mat],
        out_shape=[jax.ShapeDtypeStruct((b, t, RW_WIDTH), BF16)] * 2
        + [jax.ShapeDtypeStruct((2, b, t, RW_WIDTH), BF16)] * 2
        + [jax.ShapeDtypeStruct((2, b, nc, HEAD_DIM, RW_WIDTH), BF16)] * 2,
        compiler_params=pltpu.CompilerParams(dimension_semantics=("arbitrary", "arbitrary"),
                                             vmem_limit_bytes=VMEM_LIMIT),
        name="rw_pre",
    )(prw, *consts)


def _rw_scan_kernel(rhf_ref, y0f_ref, mf_ref, nf_ref, rhr_ref, y0r_ref, mr_ref, nr_ref, z0_ref, bd_ref,
                    yf_ref, yr_ref, zf_ref, z_scr):
    j = pl.program_id(0)

    @pl.when(j == 0)
    def _():
        z_scr[...] = z0_ref[...]

    cl = CHUNK
    nb = z_scr.shape[1]
    n_ch = rhf_ref.shape[2] // cl
    bd_mask = bd_ref[...]
    grp = bd_mask.shape[0]
    dirs = ((rhf_ref, y0f_ref, mf_ref, nf_ref, yf_ref), (rhr_ref, y0r_ref, mr_ref, nr_ref, yr_ref))
    for step in range(n_ch):
        for d, (rh_ref, y0_ref, m_ref, n_ref, y_ref) in enumerate(dirs):
            ch = step if d == 0 else n_ch - 1 - step
            rows = slice(ch * cl, (ch + 1) * cl)
            for bi in range(nb):
                for g in range(RW_WIDTH // grp):
                    gs = slice(g * grp, (g + 1) * grp)
                    z_bd = jnp.concatenate([z_scr[d, bi, :, gs].astype(BF16)] * (grp // HEAD_DIM), axis=0) * bd_mask
                    lhs = jnp.concatenate([rh_ref[0, bi, rows, gs], m_ref[0, bi, ch, :, gs]], axis=0)
                    res = _dot(lhs, z_bd)
                    y_ref[bi, rows, gs] = (res[:cl] + y0_ref[0, bi, rows, gs]).astype(y_ref.dtype)
                    z_scr[d, bi, :, gs] = res[cl:] + n_ref[0, bi, ch, :, gs]

    @pl.when(j == pl.num_programs(0) - 1)
    def _():
        zf_ref[...] = z_scr[...]


def _rw_scan(rh, y0, m, n, z0, bd_mask):
    _, b, t, _ = rh.shape
    cl = CHUNK
    cps = RW_SCAN_CHUNKS_PER_STEP
    tm = cps * cl
    ns = t // tm
    assert t % tm == 0
    tokf = pl.BlockSpec((1, b, tm, RW_WIDTH), lambda j: (0, 0, j, 0))
    tokr = pl.BlockSpec((1, b, tm, RW_WIDTH), lambda j: (1, 0, ns - 1 - j, 0))
    matf = pl.BlockSpec((1, b, cps, HEAD_DIM, RW_WIDTH), lambda j: (0, 0, j, 0, 0))
    matr = pl.BlockSpec((1, b, cps, HEAD_DIM, RW_WIDTH), lambda j: (1, 0, ns - 1 - j, 0, 0))
    zspec = _const_spec(z0.shape)
    return pl.pallas_call(
        _rw_scan_kernel,
        grid=(ns,),
        in_specs=[tokf, tokf, matf, matf, tokr, tokr, matr, matr, zspec, _const_spec(bd_mask.shape)],
        out_specs=[pl.BlockSpec((b, tm, RW_WIDTH), lambda j: (0, j, 0)),
                   pl.BlockSpec((b, tm, RW_WIDTH), lambda j: (0, ns - 1 - j, 0)),
                   zspec],
        out_shape=[jax.ShapeDtypeStruct((b, t, RW_WIDTH), BF16)] * 2 + [jax.ShapeDtypeStruct(z0.shape, F32)],
        scratch_shapes=[pltpu.VMEM(z0.shape, F32)],
        compiler_params=pltpu.CompilerParams(dimension_semantics=("arbitrary",), vmem_limit_bytes=VMEM_LIMIT),
        name="rw_scan",
    )(rh, y0, m, n, rh, y0, m, n, z0, bd_mask)


def _tail_kernel(yf_ref, yr_ref, bon_ref, g_ref, ona_ref, x_ref, gate1_ref, sh_ref, sc_ref, gate2_ref,
                 lng_ref, lnb_ref, ones_ref, wona_ref, worw_ref, g2_ref, w1g_ref, w1u_ref, w2_ref, o_ref, *, ff_chunk):
    ones = ones_ref[...]
    wkv = yf_ref[0].astype(F32) + yr_ref[0].astype(F32)
    mu = _head_sum(wkv, ones, split=False) * (1.0 / HEAD_DIM)
    dlt = wkv - mu
    var = _head_sum(dlt * dlt, ones, split=False) * (1.0 / HEAD_DIM)
    yn = dlt * lax.rsqrt(var + RW_GN_EPS) * lng_ref[...] + lnb_ref[...]
    orw = ((yn + bon_ref[0]) * g_ref[0]).astype(BF16)
    mix = _dot(ona_ref[0], wona_ref[...]) + _dot(orw, worw_ref[...])
    h1 = x_ref[0] + gate1_ref[0] * mix
    ms = jnp.mean(h1 * h1, axis=-1, keepdims=True)
    u = (h1 * lax.rsqrt(ms + NORM_EPS) * g2_ref[...] * (1.0 + sc_ref[0]) + sh_ref[0]).astype(BF16)
    bounds = _ff_bounds(w2_ref.shape[0], ff_chunk)

    def up_proj(lo, hi):
        return _dot(u, w1g_ref[:, lo:hi]), _dot(u, w1u_ref[:, lo:hi])

    acc = jnp.zeros(h1.shape, F32)
    nxt = up_proj(*bounds[0])
    for ci, (lo, hi) in enumerate(bounds):
        gt, up = nxt
        if ci + 1 < len(bounds):
            nxt = up_proj(*bounds[ci + 1])
        act = (gt * _sigmoid(gt) * up).astype(BF16)
        acc = acc + _dot(act, w2_ref[lo:hi, :])
    o_ref[0] = h1 + gate2_ref[0] * acc


FF_CHUNK = 512


def _ff_bounds(d_ff, ff_chunk):
    edges = list(range(0, d_ff, ff_chunk)) + [d_ff]
    return [(lo, hi) for lo, hi in zip(edges[:-1], edges[1:])]


def _tail(yf, yr, bon, g, ona, x, gate1, sh, sc, gate2, lng, lnb, ones_bd, wona, worw, g2, w1g, w1u, w2):
    b, t, d = x.shape
    tm = min(512, t)
    vec = pl.BlockSpec((1, 1, d), lambda bi, i: (bi, 0, 0))
    tok = lambda w: pl.BlockSpec((1, tm, w), lambda bi, i: (bi, i, 0))
    consts = (lng, lnb, ones_bd, wona, worw, g2, w1g, w1u, w2)
    return pl.pallas_call(
        functools.partial(_tail_kernel, ff_chunk=FF_CHUNK),
        grid=(b, t // tm),
        in_specs=[tok(RW_WIDTH)] * 4 + [tok(NA_WIDTH), tok(d), vec, vec, vec, vec]
        + [_const_spec(a.shape, single=True) for a in consts],
        out_specs=tok(d),
        out_shape=jax.ShapeDtypeStruct((b, t, d), F32),
        compiler_params=pltpu.CompilerParams(dimension_semantics=("arbitrary", "arbitrary"),
                                             vmem_limit_bytes=VMEM_LIMIT),
        name="tail",
    )(yf, yr, bon, g, ona, x, gate1, sh, sc, gate2, *consts)


def _pad_cols(w, width):
    return jnp.pad(w, ((0, 0), (0, width - w.shape[1])))


def _rw_layout(w):
    o = 3 * RW_WIDTH
    wd = w[:, o:o + 2 * DECAY_LORA]
    ad = _pad_cols(w[:, o + 2 * DECAY_LORA:o + 2 * DECAY_LORA + AAA_LORA], LANE)
    gd = w[:, o + 2 * DECAY_LORA + AAA_LORA:]
    return jnp.concatenate([w[:, :o], wd, ad, gd], axis=1)


def _hl(w):
    hi = w.astype(BF16)
    return hi, (w - hi.astype(F32)).astype(BF16)


def kernel(x, c, ctx, c_ctx, norm1_g, norm2_g, w_ada, b_ada, w_in, na_q_g, na_k_g, na_rpb, rw_mu_prev, rw_mu_next,
           rw_w0, rw_w_up, rw_a0, rw_a_up, rw_g_up, rw_k_k, rw_k_a, rw_r_k, rw_ln_g, rw_ln_b, w_out, ffn_w_in,
           ffn_w_out):
    depth = w_in.shape[0]
    assert depth == 1, "single-layer kernel"
    b, t, d = x.shape
    l_ctx = ctx.shape[1]
    lyr = 0

    n_rows = -(-(b + 1) // 8) * 8
    c_rows = jnp.zeros((n_rows, d), F32).at[:b].set(c).at[b].set(c_ctx)
    mod_all = _ada(c_rows, w_ada[lyr], b_ada[lyr][None, :])
    mod = [mod_all[:b, i * d:(i + 1) * d][:, None, :] for i in range(6)]
    modc = [jnp.broadcast_to(mod_all[b, i * d:(i + 1) * d][None, None, :], (b, 1, d)) for i in range(2)]

    w_na = w_in[lyr][:, :3 * NA_WIDTH].astype(BF16)
    w_rw = _rw_layout(w_in[lyr][:, 3 * NA_WIDTH:]).astype(BF16)
    hd = np.arange(4 * HEAD_DIM) // HEAD_DIM
    ones_bd = jnp.asarray(hd[:, None] == hd[None, :], BF16)
    qg = jnp.tile(na_q_g[lyr], NA_HEADS)[None, :]
    kg = jnp.tile(na_k_g[lyr], NA_HEADS)[None, :]
    g1 = norm1_g[lyr][None, :]

    mu_p = _rw_layout(rw_mu_prev[lyr][None, :])
    mu_n = _rw_layout(rw_mu_next[lyr][None, :])
    q, k, v, prw = _inproj(x, mod[0], mod[1], g1, w_na, w_rw, qg, kg, ones_bd, mu_p, mu_n)
    _, kc, vc, prw_c = _inproj(ctx, modc[0], modc[1], g1, w_na, w_rw, qg, kg, ones_bd, mu_p, mu_n)

    bias = _na_bias_tables(na_rpb[lyr], t // GRID_W, NA_ROWS_PER_STEP)
    o_na = _na(q, k, v, kc, vc, bias)

    zeros_up = jnp.zeros((DECAY_LORA, RW_WIDTH), F32)
    wup = jnp.concatenate([jnp.concatenate([rw_w_up[lyr, 0], zeros_up], axis=1),
                           jnp.concatenate([zeros_up, rw_w_up[lyr, 1]], axis=1)], axis=0)
    aup = jnp.concatenate([jnp.concatenate([rw_a_up[lyr, 0], rw_a_up[lyr, 1]], axis=1),
                           jnp.zeros((LANE - AAA_LORA, 2 * RW_WIDTH), F32)], axis=0)
    ri = np.arange(RW_CHUNKS_PER_STEP * CHUNK)
    same = (ri[None, :] // CHUNK) == (ri[:, None] // CHUNK)
    tri = jnp.asarray(np.stack([same & (ri[None, :] <= ri[:, None]), same & (ri[None, :] >= ri[:, None])]), BF16)
    consts = (rw_w0[lyr], *_hl(wup), rw_a0[lyr], aup.astype(BF16), rw_g_up[lyr].astype(BF16),
              rw_k_k[lyr][None, :], rw_k_a[lyr][None, :], rw_r_k[lyr].reshape(1, RW_WIDTH), tri, ones_bd)

    _, _, rh_c, y0_c, m_c, n_c = _rw_pre(prw_c, consts)
    g, bon, rh, y0, m, n = _rw_pre(prw, consts)
    z0 = jnp.zeros((2, b, HEAD_DIM, RW_WIDTH), F32)
    _, _, z_ctx = _rw_scan(rh_c, y0_c, m_c, n_c, z0, ones_bd)
    yf, yr, _ = _rw_scan(rh, y0, m, n, z_ctx, ones_bd)

    wo = w_out[lyr].astype(BF16)
    d_ff = ffn_w_out.shape[1]
    w1 = ffn_w_in[lyr].astype(BF16)
    return _tail(yf, yr, bon, g, o_na, x, mod[2], mod[3], mod[4], mod[5],
                 rw_ln_g[lyr][None, :], rw_ln_b[lyr][None, :], ones_bd, wo[:NA_WIDTH], wo[NA_WIDTH:],
                 norm2_g[lyr][None, :], w1[:, :d_ff], w1[:, d_ff:], ffn_w_out[lyr].astype(BF16))
```

```python
import functools

import numpy as np
import jax
import jax.numpy as jnp
from jax import lax
from jax.experimental import pallas as pl
from jax.experimental.pallas import tpu as pltpu

F32 = jnp.float32
BF16 = jnp.bfloat16

GRID_W = 64
NA_HEADS = 8
HEAD_DIM = 64
NA_WIDTH = NA_HEADS * HEAD_DIM
NA_WIN_ROWS = 8
NA_WIN_COLS = 16
RW_HEADS = 8
RW_WIDTH = RW_HEADS * HEAD_DIM
DECAY_LORA = 64
AAA_LORA = 64
GATE_LORA = 128
NORM_EPS = 1e-6
RW_GN_EPS = 64e-5

LANE = 128
RW_COLS = 3 * RW_WIDTH + 3 * LANE
OFF_WD = 3 * RW_WIDTH
OFF_AD = OFF_WD + LANE
OFF_GD = OFF_AD + LANE
CHUNK = 64
INV_BASE = 16
RW_SCAN_CHUNKS_PER_STEP = 4
RW_CHUNKS_PER_STEP = 4
NA_ROWS_PER_STEP = 4
NA_HEADS_PER_STEP = 4
NEG_BIG = -1e30
VMEM_LIMIT = 56 * 1024 * 1024


def _dot(a, b):
    return jnp.dot(a, b, preferred_element_type=F32)


def _bdot(a, b):
    return jnp.dot(a.astype(BF16), b.astype(BF16), preferred_element_type=F32)


def _bdot_nt(a, b):
    return lax.dot_general(a.astype(BF16), b.astype(BF16), (((1,), (1,)), ((), ())),
                           preferred_element_type=F32)


def _bdot_tn(a, b):
    return lax.dot_general(a.astype(BF16), b.astype(BF16), (((0,), (0,)), ((), ())),
                           preferred_element_type=F32)


def _split2(a):
    hi = a.astype(BF16)
    lo = (a - hi.astype(F32)).astype(BF16)
    return hi, lo


def _head_sum(x, ones_grp, split=True):
    m, w = x.shape
    grp = ones_grp.shape[0]
    n = w // grp
    pieces = _split2(x) if split else (x.astype(BF16),)
    parts = [t[:, g * grp:(g + 1) * grp] for t in pieces for g in range(n)]
    res = _dot(jnp.concatenate(parts, axis=0), ones_grp)
    cols = []
    for g in range(n):
        col = res[g * m:(g + 1) * m]
        if split:
            col = col + res[(n + g) * m:(n + g + 1) * m]
        cols.append(col)
    return jnp.concatenate(cols, axis=1)


def _dot_hl3(a, w_hi, w_lo):
    m = a.shape[0]
    hi, lo = _split2(a)
    res = _dot(jnp.concatenate([hi, lo], axis=0), w_hi)
    return res[:m] + (res[m:] + _dot(hi, w_lo))


def _sigmoid(x):
    return jax.nn.sigmoid(x)


def _const_spec(shape, single=False):
    nd = len(shape)
    if single:
        return pl.BlockSpec(shape, lambda *_: (0,) * nd, pipeline_mode=pl.Buffered(1))
    return pl.BlockSpec(shape, lambda *_: (0,) * nd)


def _ada_kernel(c_ref, w_ref, b_ref, o_ref):
    c = c_ref[...]
    s = c * _sigmoid(c)
    w_hi, w_lo = _split2(w_ref[...])
    o_ref[...] = _dot_hl3(s, w_hi, w_lo) + b_ref[...]


def _ada(c_rows, w_ada, b_ada):
    rows, d = c_rows.shape
    n = w_ada.shape[1]
    tn = 1024
    return pl.pallas_call(
        _ada_kernel,
        grid=(n // tn,),
        in_specs=[pl.BlockSpec((rows, d), lambda j: (0, 0)),
                  pl.BlockSpec((d, tn), lambda j: (0, j)),
                  pl.BlockSpec((1, tn), lambda j: (0, j))],
        out_specs=pl.BlockSpec((rows, tn), lambda j: (0, j)),
        out_shape=jax.ShapeDtypeStruct((rows, n), F32),
        compiler_params=pltpu.CompilerParams(dimension_semantics=("arbitrary",), vmem_limit_bytes=VMEM_LIMIT),
        name="ada",
    )(c_rows, w_ada, b_ada)


def _inproj_kernel(x_ref, xp_ref, xn_ref, sh_ref, sc_ref, g_ref, wna_ref, wrw_ref, qg_ref, kg_ref, ones_ref,
                   mup_ref, mun_ref, q_ref, k_ref, v_ref, prw_ref):
    i = pl.program_id(1)

    def modulated(xt):
        ms = jnp.mean(xt * xt, axis=-1, keepdims=True)
        y = xt * lax.rsqrt(ms + NORM_EPS) * g_ref[...]
        return (y * (1.0 + sc_ref[0]) + sh_ref[0]).astype(BF16)

    u = modulated(x_ref[0])

    tm = u.shape[0]
    u_halo = modulated(jnp.concatenate([xp_ref[0], xn_ref[0]], axis=0))
    p_all = _dot(jnp.concatenate([u, u_halo], axis=0), wrw_ref[...])
    p = p_all[:tm]
    ph = p_all[tm:]
    halo_prev = ph[7:8] * (i > 0).astype(F32)
    halo_next = ph[8:9] * (i < pl.num_programs(1) - 1).astype(F32)
    row = lax.broadcasted_iota(jnp.int32, (tm, 1), 0)
    prev = jnp.where(row == 0, halo_prev, pltpu.roll(p, 1, 0))
    nxt = jnp.where(row == tm - 1, halo_next, pltpu.roll(p, tm - 1, 0))
    prw_ref[0] = (p + mup_ref[...] * (prev - p) + mun_ref[...] * (nxt - p)).astype(prw_ref.dtype)

    pn = _dot(u, wna_ref[...])

    def head_norm(t, gain):
        ss = _head_sum(t * t, ones_ref[...], split=False) * (1.0 / HEAD_DIM)
        return t * lax.rsqrt(ss + NORM_EPS) * gain

    q = head_norm(pn[:, :NA_WIDTH], qg_ref[...]) * (HEAD_DIM ** -0.5)
    k = head_norm(pn[:, NA_WIDTH:2 * NA_WIDTH], kg_ref[...])
    q_ref[0] = q.astype(BF16)
    k_ref[0] = k.astype(BF16)
    v_ref[0] = pn[:, 2 * NA_WIDTH:].astype(BF16)


def _inproj(x, shift, scale, g1, w_na, w_rw, qg, kg, ones_bd, mu_p, mu_n):
    b, t, d = x.shape
    tm = min(512, t)
    sub = tm // 8
    nb8 = t // 8
    vec = pl.BlockSpec((1, 1, d), lambda bi, i: (bi, 0, 0))
    tok = lambda w: pl.BlockSpec((1, tm, w), lambda bi, i: (bi, i, 0))
    return pl.pallas_call(
        _inproj_kernel,
        grid=(b, t // tm),
        in_specs=[tok(d),
                  pl.BlockSpec((1, 8, d), lambda bi, i: (bi, jnp.maximum(i * sub - 1, 0), 0)),
                  pl.BlockSpec((1, 8, d), lambda bi, i: (bi, jnp.minimum((i + 1) * sub, nb8 - 1), 0)),
                  vec, vec, _const_spec((1, d)), _const_spec(w_na.shape), _const_spec(w_rw.shape),
                  _const_spec((1, NA_WIDTH)), _const_spec((1, NA_WIDTH)), _const_spec(ones_bd.shape),
                  _const_spec(mu_p.shape), _const_spec(mu_n.shape)],
        out_specs=[tok(NA_WIDTH), tok(NA_WIDTH), tok(NA_WIDTH), tok(RW_COLS)],
        out_shape=[jax.ShapeDtypeStruct((b, t, NA_WIDTH), BF16)] * 3
        + [jax.ShapeDtypeStruct((b, t, RW_COLS), BF16)],
        compiler_params=pltpu.CompilerParams(dimension_semantics=("arbitrary", "arbitrary"),
                                             vmem_limit_bytes=VMEM_LIMIT),
        name="inproj",
    )(x, x, x, shift, scale, g1, w_na, w_rw, qg, kg, ones_bd, mu_p, mu_n)


def _na_bias_tables(rpb, rows, r_blk):
    kr_n = r_blk + NA_WIN_ROWS - 1
    nblk = rows // r_blk
    n_h, n_di, n_dj = rpb.shape
    w = GRID_W
    lo = w - NA_WIN_COLS
    strip = jnp.pad(rpb, ((0, 0), (0, 0), (lo, 2 * w - 1 - n_dj - lo)))
    skew = jnp.tile(strip, (1, 1, w + 1))[:, :, :2 * w * w].reshape(n_h, n_di, w, 2 * w)[..., :w]
    toe = skew[:, :, ::-1, :]
    j = np.arange(w)
    cs = np.clip(j - NA_WIN_COLS // 2, 0, w - NA_WIN_COLS)
    c = np.arange(w)
    cvalid = (c[None, :] >= cs[:, None]) & (c[None, :] < cs[:, None] + NA_WIN_COLS)
    toe = jnp.where(cvalid[None, None], toe, NEG_BIG)
    neg = jnp.full((n_h, w, w), NEG_BIG, F32)
    tabs = []
    for ib in (0, min(1, nblk - 1), nblk - 1):
        i0 = ib * r_blk
        base = int(np.clip(i0 - NA_WIN_ROWS // 2, 0, rows - kr_n))
        q_rows = []
        for i in range(i0, i0 + r_blk):
            rs = int(np.clip(i - NA_WIN_ROWS // 2, 0, rows - NA_WIN_ROWS))
            blocks = []
            for kr in range(base, base + kr_n):
                blocks.append(toe[:, kr - i + NA_WIN_ROWS - 1] if rs <= kr < rs + NA_WIN_ROWS else neg)
            q_rows.append(jnp.concatenate(blocks, axis=-1))
        tabs.append(jnp.concatenate(q_rows, axis=1))
    return jnp.stack(tabs)


def _na_kernel(q_ref, k_ref, v_ref, kc_ref, vc_ref, bias_ref, o_ref, *, r_blk, rows):
    i = pl.program_id(2)
    kr_n = r_blk + NA_WIN_ROWS - 1
    base = jnp.clip(i * r_blk - NA_WIN_ROWS // 2, 0, rows - kr_n)
    start = pl.multiple_of(base * GRID_W, GRID_W)
    n_pair = q_ref.shape[2] // LANE
    nq = q_ref.shape[1]
    lane = lax.broadcasted_iota(jnp.int32, (nq, LANE), 1)
    psl = lambda pr: slice(pr * LANE, (pr + 1) * LANE)
    s, sc = [], []
    for pr in range(n_pair):
        q = q_ref[0, :, psl(pr)]
        q2 = jnp.concatenate([jnp.where(lane < HEAD_DIM, q, jnp.zeros_like(q)),
                              jnp.where(lane >= HEAD_DIM, q, jnp.zeros_like(q))], axis=0)
        bias2 = jnp.concatenate([bias_ref[0, 2 * pr], bias_ref[0, 2 * pr + 1]], axis=0)
        s.append(_bdot_nt(q2, k_ref[0, pl.ds(start, kr_n * GRID_W), psl(pr)]) + bias2)
        sc.append(_bdot_nt(q2, kc_ref[0, :, psl(pr)]))
    for pr in range(n_pair):
        m = jnp.maximum(jnp.max(s[pr], axis=-1, keepdims=True), jnp.max(sc[pr], axis=-1, keepdims=True))
        p = jnp.exp(s[pr] - m)
        pc = jnp.exp(sc[pr] - m)
        l = jnp.sum(p, axis=-1, keepdims=True) + jnp.sum(pc, axis=-1, keepdims=True)
        o = _bdot(p, v_ref[0, pl.ds(start, kr_n * GRID_W), psl(pr)]) + _bdot(pc, vc_ref[0, :, psl(pr)])
        o = o / l
        o_ref[0, :, psl(pr)] = jnp.where(lane < HEAD_DIM, o[:nq], o[nq:]).astype(o_ref.dtype)


def _na(q, k, v, kc, vc, bias):
    b, t, _ = q.shape
    l_ctx = kc.shape[1]
    rows = t // GRID_W
    r_blk = NA_ROWS_PER_STEP
    kr_n = r_blk + NA_WIN_ROWS - 1
    nblk = rows // r_blk
    assert rows % r_blk == 0 and rows >= kr_n and t % GRID_W == 0
    w = NA_HEADS_PER_STEP * HEAD_DIM
    n_grp = NA_WIDTH // w

    def bias_map(bi, hg, i):
        case = jnp.where(i == 0, 0, jnp.where(i == nblk - 1, 2, 1))
        return (case, hg, 0, 0)

    whole = lambda rows_: pl.BlockSpec((1, rows_, w), lambda bi, hg, i: (bi, 0, hg))
    return pl.pallas_call(
        functools.partial(_na_kernel, r_blk=r_blk, rows=rows),
        grid=(b, n_grp, nblk),
        in_specs=[pl.BlockSpec((1, r_blk * GRID_W, w), lambda bi, hg, i: (bi, i, hg)),
                  whole(t), whole(t), whole(l_ctx), whole(l_ctx),
                  pl.BlockSpec((1, NA_HEADS_PER_STEP, r_blk * GRID_W, kr_n * GRID_W), bias_map)],
        out_specs=pl.BlockSpec((1, r_blk * GRID_W, w), lambda bi, hg, i: (bi, i, hg)),
        out_shape=jax.ShapeDtypeStruct((b, t, NA_WIDTH), BF16),
        compiler_params=pltpu.CompilerParams(dimension_semantics=("arbitrary",) * 3, vmem_limit_bytes=VMEM_LIMIT),
        name="na",
    )(q, k, v, kc, vc, bias)


def _rw_pre_kernel(p_ref, w0_ref, wup_hi_ref, wup_lo_ref, a0_ref, aup_ref,
                   gup_ref, kkw_ref, ka_ref, rk_ref, tri_ref, bd_ref,
                   g_ref, bon_ref, rh_ref, y0_ref, m_ref, n_ref):
    cl = CHUNK
    ps = p_ref[0].astype(F32)
    tm = ps.shape[0]

    r = ps[:, 0:RW_WIDTH]
    k = ps[:, RW_WIDTH:2 * RW_WIDTH]
    v = ps[:, 2 * RW_WIDTH:3 * RW_WIDTH]
    wd = ps[:, OFF_WD:OFF_WD + LANE]
    ad = ps[:, OFF_AD:OFF_AD + LANE]
    gd = ps[:, OFF_GD:OFF_GD + LANE]

    lora_w = _dot_hl3(jnp.tanh(wd), wup_hi_ref[...], wup_lo_ref[...])
    lora_a = _bdot(ad, aup_ref[...])
    g_ref[0] = _bdot(_sigmoid(gd), gup_ref[...]).astype(g_ref.dtype)

    bd_mask = bd_ref[...]
    kkv = k * kkw_ref[...]
    kk = kkv * lax.rsqrt(jnp.maximum(_head_sum(kkv * kkv, bd_mask, split=False), 1e-24))

    grp = 4 * HEAD_DIM
    n_grp = RW_WIDTH // grp
    ri = lax.broadcasted_iota(jnp.int32, (cl, grp), 0)
    ci = lax.broadcasted_iota(jnp.int32, (cl, grp), 1) & (HEAD_DIM - 1)
    eye_f = (ri == ci).astype(F32)
    n_sq = int(np.log2(INV_BASE)) - 1
    blk = []
    size = INV_BASE
    while size <= cl:
        sh = int(np.log2(size))
        blk.append((ri >> sh) == (ci >> sh))
        size *= 2
    strict = ((ci < ri), (ci > ri))
    incl = ((ci <= ri), (ci >= ri))

    def bd(x):
        xb = x.astype(BF16)
        return jnp.concatenate([xb] * (grp // cl), axis=0) * bd_mask

    n_ch = tm // cl
    csl = lambda ch: slice(ch * cl, (ch + 1) * cl)
    bonus_terms = jnp.zeros((tm, RW_WIDTH), F32)
    per_dir = []
    for d in range(2):
        sl_d = slice(d * RW_WIDTH, (d + 1) * RW_WIDTH)
        z = w0_ref[d:d + 1, :] + lora_w[:, sl_d]
        lw = _sigmoid(z) * (-float(np.exp(-0.5)))
        a = _sigmoid(a0_ref[d:d + 1, :] + lora_a[:, sl_d])
        kd = k * (1.0 + (a - 1.0) * ka_ref[...])
        bvec = kk * a
        bonus_terms = bonus_terms + r * kd * rk_ref[...]

        l_hi, l_lo = _split2(lw)
        tri = tri_ref[d]
        cum = _dot(tri, l_hi) + _dot(tri, l_lo)
        tot = [jnp.sum(lw[csl(ch)], axis=0, keepdims=True) for ch in range(n_ch)]
        tot_rows = jnp.concatenate([jnp.broadcast_to(tc, (cl, RW_WIDTH)) for tc in tot], axis=0)
        e_neg = jnp.exp(-cum)
        e_tot = jnp.exp(tot_rows - cum)
        per_dir.append(dict(at=(-kk * jnp.exp(cum - lw)).astype(BF16), rt=r * jnp.exp(cum),
                            bt=(bvec * e_neg).astype(BF16), kt=(kd * e_neg).astype(BF16),
                            bb=(bvec * e_tot).astype(BF16), kb=(kd * e_tot).astype(BF16),
                            p_tot=[jnp.exp(tc) for tc in tot]))
    bon_ref[0] = (_head_sum(bonus_terms, bd_mask, split=False) * v).astype(bon_ref.dtype)
    v_bf = v.astype(BF16)

    probs = [(ch, d, g) for ch in range(n_ch) for d in range(2) for g in range(n_grp)]
    rng = range(len(probs))
    gsl = lambda g: slice(g * grp, (g + 1) * grp)
    at = [per_dir[d]["at"][csl(ch), gsl(g)] for ch, d, g in probs]
    rt = [per_dir[d]["rt"][csl(ch), gsl(g)] for ch, d, g in probs]
    bd_v = {(ch, g): bd(v_bf[csl(ch), gsl(g)]) for ch in range(n_ch) for g in range(n_grp)}
    lhs = [jnp.concatenate([at[i], rt[i].astype(BF16)], axis=0) for i in rng]
    s_b = [_bdot_nt(lhs[i], bd(per_dir[d]["bt"][csl(ch), gsl(g)])) for i, (ch, d, g) in enumerate(probs)]
    s_k = [_bdot_nt(lhs[i], bd(per_dir[d]["kt"][csl(ch), gsl(g)])) for i, (ch, d, g) in enumerate(probs)]
    aab = [jnp.where(strict[probs[i][1]], s_b[i][:cl], 0.0) for i in rng]
    arb = [jnp.where(incl[probs[i][1]], s_b[i][cl:], 0.0).astype(BF16) for i in rng]
    aak = [jnp.where(strict[probs[i][1]], s_k[i][:cl], 0.0).astype(BF16) for i in rng]
    ark = [jnp.where(incl[probs[i][1]], s_k[i][cl:], 0.0).astype(BF16) for i in rng]
    xin = [jnp.where(blk[0], aab[i], 0.0) for i in rng]
    apow = [_bdot(xin[i], bd(xin[i])) for i in rng]
    for _ in range(n_sq - 1):
        res = [_bdot(jnp.concatenate([xin[i], apow[i]], axis=0), bd(apow[i])) for i in rng]
        xin = [xin[i] + (apow[i] + res[i][:cl]) for i in rng]
        apow = [res[i][cl:] for i in rng]
    xin = [xin[i] + (apow[i] + _bdot(xin[i], bd(apow[i]))) for i in rng]
    for lvl in range(1, len(blk)):
        sel = jnp.logical_and(blk[lvl], jnp.logical_not(blk[lvl - 1]))
        off = [jnp.where(sel, aab[i], 0.0) for i in rng]
        tmp = [off[i] + _bdot(off[i], bd(xin[i])) for i in rng]
        xin = [xin[i] + (tmp[i] + _bdot(xin[i], bd(tmp[i]))) for i in rng]
    xin = [xin[i].astype(BF16) for i in rng]
    ah = [(at[i].astype(F32) + _bdot(xin[i], bd(at[i]))).astype(BF16) for i in rng]
    av = [_bdot(aak[i], bd_v[probs[i][0], probs[i][2]]) for i in rng]
    wh = [(av[i] + _bdot(xin[i], bd(av[i]))).astype(BF16) for i in rng]
    for i, (ch, d, g) in enumerate(probs):
        rh_ref[d, 0, csl(ch), gsl(g)] = (rt[i] + _bdot(arb[i], bd(ah[i]))).astype(rh_ref.dtype)
    for i, (ch, d, g) in enumerate(probs):
        y0_ref[d, 0, csl(ch), gsl(g)] = (_bdot(arb[i], bd(wh[i])) + _bdot(ark[i], bd_v[ch, g])).astype(y0_ref.dtype)
    r2 = lax.broadcasted_iota(jnp.int32, (grp, grp), 0) >> int(np.log2(HEAD_DIM))
    c2 = lax.broadcasted_iota(jnp.int32, (grp, grp), 1) >> int(np.log2(HEAD_DIM))
    same_head = r2 == c2

    def fold(full):
        fm = jnp.where(same_head, full, 0.0)
        out = fm[0:HEAD_DIM]
        for j in range(1, grp // HEAD_DIM):
            out = out + fm[j * HEAD_DIM:(j + 1) * HEAD_DIM]
        return out

    for i, (ch, d, g) in enumerate(probs):
        bb = per_dir[d]["bb"][csl(ch), gsl(g)]
        diag = eye_f * per_dir[d]["p_tot"][ch][:, gsl(g)]
        m_ref[d, 0, ch, :, gsl(g)] = (diag + fold(_bdot_tn(bb, ah[i]))).astype(m_ref.dtype)
    for i, (ch, d, g) in enumerate(probs):
        bk = jnp.concatenate([per_dir[d]["bb"][csl(ch), gsl(g)], per_dir[d]["kb"][csl(ch), gsl(g)]], axis=0)
        wv = jnp.concatenate([wh[i], v_bf[csl(ch), gsl(g)]], axis=0)
        n_ref[d, 0, ch, :, gsl(g)] = fold(_bdot_tn(bk, wv)).astype(n_ref.dtype)


def _rw_pre(prw, consts):
    b, t, _ = prw.shape
    cl = CHUNK
    cps = RW_CHUNKS_PER_STEP
    tm = cps * cl
    nc = t // cl
    assert t % tm == 0 and tm % 8 == 0
    tok = lambda w: pl.BlockSpec((1, tm, w), lambda bi, c: (bi, c, 0))
    in_specs = [tok(RW_COLS)] + [_const_spec(a.shape) for a in consts]
    dirtok = pl.BlockSpec((2, 1, tm, RW_WIDTH), lambda bi, c: (0, bi, c, 0))
    mat = pl.BlockSpec((2, 1, cps, HEAD_DIM, RW_WIDTH), lambda bi, c: (0, bi, c, 0, 0))
    return pl.pallas_call(
        _rw_pre_kernel,
        grid=(b, t // tm),
        in_specs=in_specs,
        out_specs=[tok(RW_WIDTH), tok(RW_WIDTH), dirtok, dirtok, mat, mat],
        out_shape=[jax.ShapeDtypeStruct((b, t, RW_WIDTH), BF16)] * 2
        + [jax.ShapeDtypeStruct((2, b, t, RW_WIDTH), BF16)] * 2
        + [jax.ShapeDtypeStruct((2, b, nc, HEAD_DIM, RW_WIDTH), BF16)] * 2,
        compiler_params=pltpu.CompilerParams(dimension_semantics=("arbitrary", "arbitrary"),
                                             vmem_limit_bytes=VMEM_LIMIT),
        name="rw_pre",
    )(prw, *consts)


def _rw_scan_kernel(rhf_ref, y0f_ref, mf_ref, nf_ref, rhr_ref, y0r_ref, mr_ref, nr_ref, z0_ref, bd_ref,
                    yf_ref, yr_ref, zf_ref, z_scr):
    j = pl.program_id(0)

    @pl.when(j == 0)
    def _():
        z_scr[...] = z0_ref[...]

    cl = CHUNK
    nb = z_scr.shape[1]
    n_ch = rhf_ref.shape[2] // cl
    bd_mask = bd_ref[...]
    grp = bd_mask.shape[0]
    dirs = ((rhf_ref, y0f_ref, mf_ref, nf_ref, yf_ref), (rhr_ref, y0r_ref, mr_ref, nr_ref, yr_ref))
    for step in range(n_ch):
        for d, (rh_ref, y0_ref, m_ref, n_ref, y_ref) in enumerate(dirs):
            ch = step if d == 0 else n_ch - 1 - step
            rows = slice(ch * cl, (ch + 1) * cl)
            for bi in range(nb):
                for g in range(RW_WIDTH // grp):
                    gs = slice(g * grp, (g + 1) * grp)
                    z_bd = jnp.concatenate([z_scr[d, bi, :, gs].astype(BF16)] * (grp // HEAD_DIM), axis=0) * bd_mask
                    lhs = jnp.concatenate([rh_ref[0, bi, rows, gs], m_ref[0, bi, ch, :, gs]], axis=0)
                    res = _dot(lhs, z_bd)
                    y_ref[bi, rows, gs] = (res[:cl] + y0_ref[0, bi, rows, gs]).astype(y_ref.dtype)
                    z_scr[d, bi, :, gs] = res[cl:] + n_ref[0, bi, ch, :, gs]

    @pl.when(j == pl.num_programs(0) - 1)
    def _():
        zf_ref[...] = z_scr[...]


def _rw_scan(rh, y0, m, n, z0, bd_mask):
    _, b, t, _ = rh.shape
    cl = CHUNK
    cps = RW_SCAN_CHUNKS_PER_STEP
    tm = cps * cl
    ns = t // tm
    assert t % tm == 0
    tokf = pl.BlockSpec((1, b, tm, RW_WIDTH), lambda j: (0, 0, j, 0))
    tokr = pl.BlockSpec((1, b, tm, RW_WIDTH), lambda j: (1, 0, ns - 1 - j, 0))
    matf = pl.BlockSpec((1, b, cps, HEAD_DIM, RW_WIDTH), lambda j: (0, 0, j, 0, 0))
    matr = pl.BlockSpec((1, b, cps, HEAD_DIM, RW_WIDTH), lambda j: (1, 0, ns - 1 - j, 0, 0))
    zspec = _const_spec(z0.shape)
    return pl.pallas_call(
        _rw_scan_kernel,
        grid=(ns,),
        in_specs=[tokf, tokf, matf, matf, tokr, tokr, matr, matr, zspec, _const_spec(bd_mask.shape)],
        out_specs=[pl.BlockSpec((b, tm, RW_WIDTH), lambda j: (0, j, 0)),
                   pl.BlockSpec((b, tm, RW_WIDTH), lambda j: (0, ns - 1 - j, 0)),
                   zspec],
        out_shape=[jax.ShapeDtypeStruct((b, t, RW_WIDTH), BF16)] * 2 + [jax.ShapeDtypeStruct(z0.shape, F32)],
        scratch_shapes=[pltpu.VMEM(z0.shape, F32)],
        compiler_params=pltpu.CompilerParams(dimension_semantics=("arbitrary",), vmem_limit_bytes=VMEM_LIMIT),
        name="rw_scan",
    )(rh, y0, m, n, rh, y0, m, n, z0, bd_mask)


def _tail_kernel(yf_ref, yr_ref, bon_ref, g_ref, ona_ref, x_ref, gate1_ref, sh_ref, sc_ref, gate2_ref,
                 lng_ref, lnb_ref, ones_ref, wona_ref, worw_ref, g2_ref, w1g_ref, w1u_ref, w2_ref, o_ref, *, ff_chunk):
    ones = ones_ref[...]
    wkv = yf_ref[0].astype(F32) + yr_ref[0].astype(F32)
    mu = _head_sum(wkv, ones, split=False) * (1.0 / HEAD_DIM)
    dlt = wkv - mu
    var = _head_sum(dlt * dlt, ones, split=False) * (1.0 / HEAD_DIM)
    yn = dlt * lax.rsqrt(var + RW_GN_EPS) * lng_ref[...] + lnb_ref[...]
    orw = ((yn + bon_ref[0]) * g_ref[0]).astype(BF16)
    mix = _dot(ona_ref[0], wona_ref[...]) + _dot(orw, worw_ref[...])
    h1 = x_ref[0] + gate1_ref[0] * mix
    ms = jnp.mean(h1 * h1, axis=-1, keepdims=True)
    u = (h1 * lax.rsqrt(ms + NORM_EPS) * g2_ref[...] * (1.0 + sc_ref[0]) + sh_ref[0]).astype(BF16)
    bounds = _ff_bounds(w2_ref.shape[0], ff_chunk)

    def up_proj(lo, hi):
        return _dot(u, w1g_ref[:, lo:hi]), _dot(u, w1u_ref[:, lo:hi])

    acc = jnp.zeros(h1.shape, F32)
    nxt = up_proj(*bounds[0])
    for ci, (lo, hi) in enumerate(bounds):
        gt, up = nxt
        if ci + 1 < len(bounds):
            nxt = up_proj(*bounds[ci + 1])
        act = (gt * _sigmoid(gt) * up).astype(BF16)
        acc = acc + _dot(act, w2_ref[lo:hi, :])
    o_ref[0] = h1 + gate2_ref[0] * acc


FF_CHUNK = 512


def _ff_bounds(d_ff, ff_chunk):
    edges = list(range(0, d_ff, ff_chunk)) + [d_ff]
    return [(lo, hi) for lo, hi in zip(edges[:-1], edges[1:])]


def _tail(yf, yr, bon, g, ona, x, gate1, sh, sc, gate2, lng, lnb, ones_bd, wona, worw, g2, w1g, w1u, w2):
    b, t, d = x.shape
    tm = min(512, t)
    vec = pl.BlockSpec((1, 1, d), lambda bi, i: (bi, 0, 0))
    tok = lambda w: pl.BlockSpec((1, tm, w), lambda bi, i: (bi, i, 0))
    consts = (lng, lnb, ones_bd, wona, worw, g2, w1g, w1u, w2)
    return pl.pallas_call(
        functools.partial(_tail_kernel, ff_chunk=FF_CHUNK),
        grid=(b, t // tm),
        in_specs=[tok(RW_WIDTH)] * 4 + [tok(NA_WIDTH), tok(d), vec, vec, vec, vec]
        + [_const_spec(a.shape, single=True) for a in consts],
        out_specs=tok(d),
        out_shape=jax.ShapeDtypeStruct((b, t, d), F32),
        compiler_params=pltpu.CompilerParams(dimension_semantics=("arbitrary", "arbitrary"),
                                             vmem_limit_bytes=VMEM_LIMIT),
        name="tail",
    )(yf, yr, bon, g, ona, x, gate1, sh, sc, gate2, *consts)


def _pad_cols(w, width):
    return jnp.pad(w, ((0, 0), (0, width - w.shape[1])))


def _rw_layout(w):
    o = 3 * RW_WIDTH
    wd = w[:, o:o + 2 * DECAY_LORA]
    ad = _pad_cols(w[:, o + 2 * DECAY_LORA:o + 2 * DECAY_LORA + AAA_LORA], LANE)
    gd = w[:, o + 2 * DECAY_LORA + AAA_LORA:]
    return jnp.concatenate([w[:, :o], wd, ad, gd], axis=1)


def _hl(w):
    hi = w.astype(BF16)
    return hi, (w - hi.astype(F32)).astype(BF16)


def kernel(x, c, ctx, c_ctx, norm1_g, norm2_g, w_ada, b_ada, w_in, na_q_g, na_k_g, na_rpb, rw_mu_prev, rw_mu_next,
           rw_w0, rw_w_up, rw_a0, rw_a_up, rw_g_up, rw_k_k, rw_k_a, rw_r_k, rw_ln_g, rw_ln_b, w_out, ffn_w_in,
           ffn_w_out):
    depth = w_in.shape[0]
    assert depth == 1, "single-layer kernel"
    b, t, d = x.shape
    l_ctx = ctx.shape[1]
    lyr = 0

    n_rows = -(-(b + 1) // 8) * 8
    c_rows = jnp.zeros((n_rows, d), F32).at[:b].set(c).at[b].set(c_ctx)
    mod_all = _ada(c_rows, w_ada[lyr], b_ada[lyr][None, :])
    mod = [mod_all[:b, i * d:(i + 1) * d][:, None, :] for i in range(6)]
    modc = [jnp.broadcast_to(mod_all[b, i * d:(i + 1) * d][None, None, :], (b, 1, d)) for i in range(2)]

    w_na = w_in[lyr][:, :3 * NA_WIDTH].astype(BF16)
    w_rw = _rw_layout(w_in[lyr][:, 3 * NA_WIDTH:]).astype(BF16)
    hd = np.arange(4 * HEAD_DIM) // HEAD_DIM
    ones_bd = jnp.asarray(hd[:, None] == hd[None, :], BF16)
    qg = jnp.tile(na_q_g[lyr], NA_HEADS)[None, :]
    kg = jnp.tile(na_k_g[lyr], NA_HEADS)[None, :]
    g1 = norm1_g[lyr][None, :]

    mu_p = _rw_layout(rw_mu_prev[lyr][None, :])
    mu_n = _rw_layout(rw_mu_next[lyr][None, :])
    q, k, v, prw = _inproj(x, mod[0], mod[1], g1, w_na, w_rw, qg, kg, ones_bd, mu_p, mu_n)
    _, kc, vc, prw_c = _inproj(ctx, modc[0], modc[1], g1, w_na, w_rw, qg, kg, ones_bd, mu_p, mu_n)

    bias = _na_bias_tables(na_rpb[lyr], t // GRID_W, NA_ROWS_PER_STEP)
    o_na = _na(q, k, v, kc, vc, bias)

    zeros_up = jnp.zeros((DECAY_LORA, RW_WIDTH), F32)
    wup = jnp.concatenate([jnp.concatenate([rw_w_up[lyr, 0], zeros_up], axis=1),
                           jnp.concatenate([zeros_up, rw_w_up[lyr, 1]], axis=1)], axis=0)
    aup = jnp.concatenate([jnp.concatenate([rw_a_up[lyr, 0], rw_a_up[lyr, 1]], axis=1),
                           jnp.zeros((LANE - AAA_LORA, 2 * RW_WIDTH), F32)], axis=0)
    ri = np.arange(RW_CHUNKS_PER_STEP * CHUNK)
    same = (ri[None, :] // CHUNK) == (ri[:, None] // CHUNK)
    tri = jnp.asarray(np.stack([same & (ri[None, :] <= ri[:, None]), same & (ri[None, :] >= ri[:, None])]), BF16)
    consts = (rw_w0[lyr], *_hl(wup), rw_a0[lyr], aup.astype(BF16), rw_g_up[lyr].astype(BF16),
              rw_k_k[lyr][None, :], rw_k_a[lyr][None, :], rw_r_k[lyr].reshape(1, RW_WIDTH), tri, ones_bd)

    _, _, rh_c, y0_c, m_c, n_c = _rw_pre(prw_c, consts)
    g, bon, rh, y0, m, n = _rw_pre(prw, consts)
    z0 = jnp.zeros((2, b, HEAD_DIM, RW_WIDTH), F32)
    _, _, z_ctx = _rw_scan(rh_c, y0_c, m_c, n_c, z0, ones_bd)
    yf, yr, _ = _rw_scan(rh, y0, m, n, z_ctx, ones_bd)

    wo = w_out[lyr].astype(BF16)
    d_ff = ffn_w_out.shape[1]
    w1 = ffn_w_in[lyr].astype(BF16)
    return _tail(yf, yr, bon, g, o_na, x, mod[2], mod[3], mod[4], mod[5],
                 rw_ln_g[lyr][None, :], rw_ln_b[lyr][None, :], ones_bd, wo[:NA_WIDTH], wo[NA_WIDTH:],
                 norm2_g[lyr][None, :], w1[:, :d_ff], w1[:, d_ff:], ffn_w_out[lyr].astype(BF16))
```
